```python
import math
import jax
import jax.numpy as jnp
from jax import lax
import numpy as np

D_MODEL = 1024
BATCH = 4
SEQ = 4096
DEPTH = 4
DEC_BATCH = 32
DEC_SEQ = 4
PAST_LEN = 8192
PAGE_SIZE = 128

N_MIXERS = 3
KIND_MOBA = 0
KIND_DSA = 1
KIND_GDN = 2
N_MOBA = len(range(KIND_MOBA, DEPTH, N_MIXERS))
N_DSA = len(range(KIND_DSA, DEPTH, N_MIXERS))
N_GDN = len(range(KIND_GDN, DEPTH, N_MIXERS))
N_KV = N_MOBA + N_DSA

HEAD_DIM = 64
N_HEADS = 12
ATT_W = N_HEADS * HEAD_DIM
MEM_HEADS = 4
MEM_LEN = 256
MEM_W = MEM_HEADS * HEAD_DIM
MIX_W = ATT_W + MEM_W
ROT_DIM = HEAD_DIM // 4
ROPE_THETA = 500000.0
ATT_SCALE = HEAD_DIM ** -0.5

MOBA_BLOCK = 256
MOBA_TOPK = 3
MOBA_Q_CHUNK = 32

IDX_HEADS = 8
IDX_DIM = 32
IDX_ROT = IDX_DIM // 4
IDX_SCALE = IDX_DIM ** -0.5
DSA_TOPK = 256
DSA_Q_CHUNK = 64

GDN_HEADS = 6
GDN_DK = 128
GDN_DV = 128
GDN_W = GDN_HEADS * GDN_DK
CONV_W = 4
CONV_CH = 3 * GDN_W
GDN_CHUNK = 64

D_FF = 4 * D_MODEL
EPS = 1e-6
NEG = -1e30

MOBA_IN = 3 * ATT_W + MEM_W
DSA_IN = 3 * ATT_W + IDX_HEADS * IDX_DIM + IDX_DIM + IDX_HEADS + MEM_W
GDN_IN = 4 * GDN_W + 2 * GDN_HEADS + MEM_W

kernel_name = 'hybrid_moba_dsa_gdn_memory_decoder_step'


def rms_norm(x, g):
    xf = x.astype(jnp.float32)
    y = xf * lax.rsqrt(jnp.mean(xf * xf, axis=-1, keepdims=True) + EPS)
    return (y * g.astype(jnp.float32)).astype(x.dtype)


def l2_normalize(x):
    return x * lax.rsqrt(jnp.sum(x * x, axis=-1, keepdims=True) + EPS)


def rope_tables(pos, rot_dim):
    inv_freq = ROPE_THETA ** (-jnp.arange(0, rot_dim, 2, dtype=jnp.float32) / rot_dim)
    ang = pos.astype(jnp.float32)[:, None] * inv_freq[None, :]
    return jnp.cos(ang), jnp.sin(ang)


def partial_rope(x, cos, sin):
    half = cos.shape[-1]
    xf = x.astype(jnp.float32)
    x1, x2, rest = xf[..., :half], xf[..., half:2 * half], xf[..., 2 * half:]
    c, s = cos[:, None, :], sin[:, None, :]
    return jnp.concatenate([x1 * c - x2 * s, x2 * c + x1 * s, rest], axis=-1).astype(x.dtype)


def gather_pages(pool, page_table):
    rows = pool[page_table]
    return rows.reshape(page_table.shape[0], -1, *pool.shape[2:])


def sweep_queries(fn, chunk, pos, *qs):
    t = pos.shape[0]
    if t <= chunk:
        return fn(pos, *qs)
    nc = t // chunk
    split = lambda a: jnp.moveaxis(a.reshape(a.shape[0], nc, chunk, *a.shape[2:]), 1, 0)
    out = lax.map(lambda args: fn(*args), (pos.reshape(nc, chunk),) + tuple(split(a) for a in qs))
    out = jnp.moveaxis(out, 0, 1)
    return out.reshape(out.shape[0], t, *out.shape[3:])


def attn_qkv(cols, q_gain, k_gain, cos, sin):
    n, t, _ = cols.shape
    q = cols[..., :ATT_W].reshape(n, t, N_HEADS, HEAD_DIM)
    k = cols[..., ATT_W:2 * ATT_W].reshape(n, t, N_HEADS, HEAD_DIM)
    v = cols[..., 2 * ATT_W:3 * ATT_W].reshape(n, t, N_HEADS, HEAD_DIM)
    q = partial_rope(rms_norm(q, q_gain), cos, sin)
    k = partial_rope(rms_norm(k, k_gain), cos, sin)
    return q, k, v


def moba_query(q, pos, kb, vb, kmean, n_top):
    n, nb, bs, h, dh = kb.shape
    t = q.shape[1]
    q32 = q.astype(jnp.float32)
    own = pos // MOBA_BLOCK
    gate = jnp.einsum('nthd,nbhd->nhtb', q32, kmean)
    is_past = jnp.arange(nb)[None, :] < own[:, None]
    gate = jnp.where(is_past, gate, NEG)
    _, top = lax.top_k(gate, n_top)
    ok_top = top < own[:, None]
    own_b = jnp.broadcast_to(own[:, None], top.shape[:-1] + (1,))
    sel = jnp.concatenate([top, own_b], axis=-1)
    ok = jnp.concatenate([ok_top, jnp.ones(top.shape[:-1] + (1,), dtype=bool)], axis=-1)
    n_i = jnp.arange(n)[:, None, None, None]
    h_i = jnp.arange(h)[None, :, None, None]
    gk = kb[n_i, sel, :, h_i].astype(jnp.float32)
    gv = vb[n_i, sel, :, h_i].astype(jnp.float32)
    s = jnp.einsum('nthd,nhtsjd->nhtsj', q32, gk) * ATT_SCALE
    kpos = sel[..., None] * MOBA_BLOCK + jnp.arange(bs)
    valid = ok[..., None] & (kpos <= pos[:, None, None])
    s = jnp.where(valid, s, NEG).reshape(n, h, t, -1)
    p = jax.nn.softmax(s, axis=-1)
    return jnp.einsum('nhtk,nhtkd->nthd', p, gv.reshape(n, h, t, -1, dh))


def moba_mix(q, pos, k_all, v_all):
    n, l, h, dh = k_all.shape
    nb = -(-l // MOBA_BLOCK)
    pad = ((0, 0), (0, nb * MOBA_BLOCK - l), (0, 0), (0, 0))
    kb = jnp.pad(k_all, pad).reshape(n, nb, MOBA_BLOCK, h, dh)
    vb = jnp.pad(v_all, pad).reshape(n, nb, MOBA_BLOCK, h, dh)
    kmean = jnp.mean(kb.astype(jnp.float32), axis=2)
    n_top = min(MOBA_TOPK, nb)
    fn = lambda pos_c, q_c: moba_query(q_c, pos_c, kb, vb, kmean, n_top)
    return sweep_queries(fn, MOBA_Q_CHUNK, pos, q)


def indexer_proj(cols, cos, sin):
    n, t, _ = cols.shape
    off = 3 * ATT_W
    qi = cols[..., off:off + IDX_HEADS * IDX_DIM].reshape(n, t, IDX_HEADS, IDX_DIM)
    off = off + IDX_HEADS * IDX_DIM
    ki = cols[..., off:off + IDX_DIM]
    wi = cols[..., off + IDX_DIM:off + IDX_DIM + IDX_HEADS] * IDX_HEADS ** -0.5
    qi = partial_rope(qi, cos, sin)
    ki = partial_rope(ki[:, :, None, :], cos, sin)[:, :, 0, :]
    return qi, ki, wi


def dsa_mix(q, qi, wi, pos, k_all, v_all, ki_all):
    l = k_all.shape[1]
    k_sel = min(DSA_TOPK, l // 4)
    ki32 = ki_all.astype(jnp.float32)
    kpos = jnp.arange(l)

    def block(pos_c, q_c, qi_c, wi_c):
        n = q_c.shape[0]
        isc = jnp.einsum('nthd,nsd->nths', qi_c.astype(jnp.float32), ki32) * IDX_SCALE
        isc = jnp.einsum('nths,nth->nts', jax.nn.relu(isc), wi_c.astype(jnp.float32))
        isc = jnp.where(kpos[None, :] <= pos_c[:, None], isc, NEG)
        _, sel = lax.top_k(isc, k_sel)
        ok = sel <= pos_c[:, None]
        n_i = jnp.arange(n)[:, None, None]
        gk = k_all[n_i, sel].astype(jnp.float32)
        gv = v_all[n_i, sel].astype(jnp.float32)
        s = jnp.einsum('nthd,ntkhd->nhtk', q_c.astype(jnp.float32), gk) * ATT_SCALE
        s = jnp.where(ok[:, None], s, NEG)
        p = jax.nn.softmax(s, axis=-1)
        return jnp.einsum('nhtk,ntkhd->nthd', p, gv)

    return sweep_queries(block, DSA_Q_CHUNK, pos, q, qi, wi)


def causal_conv_silu(x, w, buf):
    t = x.shape[1]
    xp = jnp.concatenate([buf.astype(x.dtype), x], axis=1).astype(jnp.float32)
    y = xp[:, 0:t] * w[0]
    for j in range(1, CONV_W):
        y = y + xp[:, j:j + t] * w[j]
    return jax.nn.silu(y), xp[:, t:].astype(x.dtype)


def gated_delta_chunked(q, k, v, beta, g, s0):
    n, t, h, dk = q.shape
    dv = v.shape[-1]
    c = min(GDN_CHUNK, t)
    nc = -(-t // c)
    padn = nc * c - t

    def to_chunks(a):
        a = jnp.pad(a, [(0, 0), (0, padn)] + [(0, 0)] * (a.ndim - 2))
        a = a.reshape(n, nc, c, *a.shape[2:])
        return jnp.moveaxis(a, (1, 3), (0, 2))

    qc, kc, vc, bc, gc = (to_chunks(a) for a in (q, k, v, beta, g))
    gcum = jnp.cumsum(gc, axis=-1)
    ii = jnp.arange(c)
    tri = ii[:, None] >= ii[None, :]
    stri = ii[:, None] > ii[None, :]
    diff = gcum[..., :, None] - gcum[..., None, :]
    decay = jnp.where(tri, jnp.exp(jnp.where(tri, diff, 0.0)), 0.0)
    kb = kc * bc[..., None]
    m = jnp.where(stri, jnp.einsum('...id,...jd->...ij', kb, kc) * decay, 0.0)
    a_mat = m + jnp.eye(c, dtype=jnp.float32)
    rhs = jnp.concatenate([vc * bc[..., None], kb * jnp.exp(gcum)[..., None]], axis=-1)
    sol = lax.linalg.triangular_solve(a_mat, rhs, left_side=True, lower=True, unit_diagonal=True)
    u, w = sol[..., :dv], sol[..., dv:]
    attn = jnp.where(tri, jnp.einsum('...id,...jd->...ij', qc, kc) * decay, 0.0)
    qd = qc * jnp.exp(gcum)[..., None]
    kd = kc * jnp.exp(gcum[..., -1:] - gcum)[..., None]
    glast = jnp.exp(gcum[..., -1])

    def step(s, xs):
        u_c, w_c, attn_c, qd_c, kd_c, gl_c = xs
        v_new = u_c - jnp.einsum('nhcd,nhde->nhce', w_c, s)
        o = jnp.einsum('nhcd,nhde->nhce', qd_c, s) + jnp.einsum('nhij,nhje->nhie', attn_c, v_new)
        s = s * gl_c[..., None, None] + jnp.einsum('nhcd,nhce->nhde', kd_c, v_new)
        return s, o

    s_t, o = lax.scan(step, s0, (u, w, attn, qd, kd, glast))
    o = jnp.moveaxis(o, (0, 2), (1, 3)).reshape(n, nc * c, h, dv)[:, :t]
    return o, s_t


def gdn_mix(cols, conv_w, a_log, dt_bias, out_norm, conv_buf, s0):
    n, t, _ = cols.shape
    qkv, new_buf = causal_conv_silu(cols[..., :CONV_CH], conv_w, conv_buf)
    z = cols[..., CONV_CH:CONV_CH + GDN_W].reshape(n, t, GDN_HEADS, GDN_DV).astype(jnp.float32)
    b = cols[..., CONV_CH + GDN_W:CONV_CH + GDN_W + GDN_HEADS].astype(jnp.float32)
    a = cols[..., CONV_CH + GDN_W + GDN_HEADS:].astype(jnp.float32)
    q = l2_normalize(qkv[..., :GDN_W].reshape(n, t, GDN_HEADS, GDN_DK)) * GDN_DK ** -0.5
    k = l2_normalize(qkv[..., GDN_W:2 * GDN_W].reshape(n, t, GDN_HEADS, GDN_DK))
    v = qkv[..., 2 * GDN_W:].reshape(n, t, GDN_HEADS, GDN_DV)
    beta = jax.nn.sigmoid(b)
    g = -jnp.exp(a_log.astype(jnp.float32)) * jax.nn.softplus(a + dt_bias.astype(jnp.float32))
    o, s_new = gated_delta_chunked(q, k, v, beta, g, s0.astype(jnp.float32))
    o = rms_norm(o, out_norm) * jax.nn.silu(z)
    return o.reshape(n, t, GDN_W), new_buf, s_new


def memory_kv(mem, w_mem_kv, mem_k_norm):
    kv = jnp.einsum('nmd,ldc->lnmc', mem, w_mem_kv)
    nl, n, m, _ = kv.shape
    mk = rms_norm(kv[..., :MEM_W].reshape(nl, n, m, MEM_HEADS, HEAD_DIM), mem_k_norm[:, None, None, None, :])
    mv = kv[..., MEM_W:].reshape(nl, n, m, MEM_HEADS, HEAD_DIM)
    return mk, mv


def mem_attend(qm, mk, mv):
    s = jnp.einsum('nthd,nmhd->nhtm', qm.astype(jnp.float32), mk.astype(jnp.float32)) * ATT_SCALE
    p = jax.nn.softmax(s, axis=-1)
    return jnp.einsum('nhtm,nmhd->nthd', p, mv.astype(jnp.float32))


def trunk(x, pos, mem_k, mem_v, past, W):
    n, t, _ = x.shape
    cos_h, sin_h = rope_tables(pos, ROT_DIM)
    cos_i, sin_i = rope_tables(pos, IDX_ROT)
    new_k, new_v, new_idx, new_delta, new_conv = [], [], [], [], []
    kind_count = [0, 0, 0]
    kv_slot = 0
    for l in range(DEPTH):
        kind = l % N_MIXERS
        j = kind_count[kind]
        kind_count[kind] += 1
        h = rms_norm(x, W['norm_mix'][l])
        w_in = (W['w_in_moba'], W['w_in_dsa'], W['w_in_gdn'])[kind][j]
        proj = h @ w_in
        cols = proj[..., :-MEM_W]
        if kind == KIND_GDN:
            if past is None:
                buf = jnp.zeros((n, CONV_W - 1, CONV_CH), x.dtype)
                s0 = jnp.zeros((n, GDN_HEADS, GDN_DK, GDN_DV), jnp.float32)
            else:
                buf = past['state_conv'][j]
                s0 = past['state_delta'][j]
            o, buf_new, s_new = gdn_mix(cols, W['gdn_conv'][j], W['gdn_a_log'][j], W['gdn_dt_bias'][j],
                                        W['gdn_out_norm'][j], buf, s0)
            new_conv.append(buf_new)
            new_delta.append(s_new)
        else:
            q, k, v = attn_qkv(cols, W['attn_q_norm'][kv_slot], W['attn_k_norm'][kv_slot], cos_h, sin_h)
            if past is None:
                k_all, v_all = k, v
            else:
                pk = gather_pages(past['cache_k'][kv_slot], past['page_table'])
                pv = gather_pages(past['cache_v'][kv_slot], past['page_table'])
                k_all = jnp.concatenate([pk, k.astype(pk.dtype)], axis=1)
                v_all = jnp.concatenate([pv, v.astype(pv.dtype)], axis=1)
            if kind == KIND_MOBA:
                o = moba_mix(q, pos, k_all, v_all)
            else:
                qi, ki, wi = indexer_proj(cols, cos_i, sin_i)
                if past is None:
                    ki_all = ki
                else:
                    pki = gather_pages(past['cache_idx'][j], past['page_table'])
                    ki_all = jnp.concatenate([pki, ki.astype(pki.dtype)], axis=1)
                o = dsa_mix(q, qi, wi, pos, k_all, v_all, ki_all)
                new_idx.append(ki)
            new_k.append(k)
            new_v.append(v)
            kv_slot += 1
            o = o.reshape(n, t, ATT_W)
        qm = rms_norm(proj[..., -MEM_W:].reshape(n, t, MEM_HEADS, HEAD_DIM), W['mem_q_norm'][l])
        om = mem_attend(qm, mem_k[l], mem_v[l]).reshape(n, t, MEM_W)
        merged = jnp.concatenate([o.astype(x.dtype), om.astype(x.dtype)], axis=-1)
        x = x + merged @ W['w_out'][l]
        h2 = rms_norm(x, W['norm_ffn'][l])
        x = x + jnp.square(jax.nn.relu(h2 @ W['w_up'][l])) @ W['w_down'][l]
    return x, jnp.stack(new_k), jnp.stack(new_v), jnp.stack(new_idx), jnp.stack(new_delta), jnp.stack(new_conv)


def setup_inputs(seed: int = 0) -> dict:
    key = jax.random.key(seed)
    ks = jax.random.split(key, 32)
    n_pages = PAST_LEN // PAGE_SIZE
    n_pool = (DEC_BATCH * n_pages * 5) // 4
    nrm = lambda k, shape, scale: jax.random.normal(k, shape, jnp.float32) * scale
    gain = lambda k, shape: 1.0 + 0.05 * jax.random.normal(k, shape, jnp.float32)
    x_prompt = nrm(ks[0], (BATCH, SEQ, D_MODEL), 1.0)
    x_sample = nrm(ks[1], (DEC_BATCH, DEC_SEQ, D_MODEL), 1.0)
    mem_prompt = nrm(ks[2], (BATCH, MEM_LEN, D_MODEL), 1.0)
    cache_k = nrm(ks[3], (N_KV, n_pool, PAGE_SIZE, N_HEADS, HEAD_DIM), 1.0)
    cache_v = nrm(ks[4], (N_KV, n_pool, PAGE_SIZE, N_HEADS, HEAD_DIM), 1.0)
    cache_idx = nrm(ks[5], (N_DSA, n_pool, PAGE_SIZE, IDX_DIM), 1.0)
    cache_mem_k = nrm(ks[6], (DEPTH, DEC_BATCH, MEM_LEN, MEM_HEADS, HEAD_DIM), 1.0)
    cache_mem_v = nrm(ks[7], (DEPTH, DEC_BATCH, MEM_LEN, MEM_HEADS, HEAD_DIM), 1.0)
    state_delta = nrm(ks[8], (N_GDN, DEC_BATCH, GDN_HEADS, GDN_DK, GDN_DV), 0.1)
    state_conv = nrm(ks[9], (N_GDN, DEC_BATCH, CONV_W - 1, CONV_CH), 1.0)
    page_table = jax.random.permutation(ks[10], n_pool)[:DEC_BATCH * n_pages]
    page_table = page_table.reshape(DEC_BATCH, n_pages).astype(jnp.int32)
    norm_mix = gain(ks[11], (DEPTH, D_MODEL))
    norm_ffn = gain(ks[12], (DEPTH, D_MODEL))
    attn_q_norm = gain(ks[13], (N_KV, HEAD_DIM))
    attn_k_norm = gain(ks[14], (N_KV, HEAD_DIM))
    mem_q_norm = gain(ks[15], (DEPTH, HEAD_DIM))
    mem_k_norm = gain(ks[16], (DEPTH, HEAD_DIM))
    w_in_moba = nrm(ks[17], (N_MOBA, D_MODEL, MOBA_IN), D_MODEL ** -0.5)
    w_in_dsa = nrm(ks[18], (N_DSA, D_MODEL, DSA_IN), D_MODEL ** -0.5)
    w_in_gdn = nrm(ks[19], (N_GDN, D_MODEL, GDN_IN), D_MODEL ** -0.5)
    gdn_conv = nrm(ks[20], (N_GDN, CONV_W, CONV_CH), CONV_W ** -0.5)
    gdn_a_log = jnp.log(jax.random.uniform(ks[21], (N_GDN, GDN_HEADS), jnp.float32, 1.0, 16.0))
    dt = jnp.exp(jax.random.uniform(ks[22], (N_GDN, GDN_HEADS), jnp.float32, math.log(1e-3), math.log(1e-1)))
    gdn_dt_bias = dt + jnp.log(-jnp.expm1(-dt))
    gdn_out_norm = gain(ks[23], (N_GDN, GDN_DV))
    w_mem_kv = nrm(ks[24], (DEPTH, D_MODEL, 2 * MEM_W), D_MODEL ** -0.5)
    w_out = nrm(ks[25], (DEPTH, MIX_W, D_MODEL), 0.5 * MIX_W ** -0.5)
    w_up = nrm(ks[26], (DEPTH, D_MODEL, D_FF), D_MODEL ** -0.5)
    w_down = nrm(ks[27], (DEPTH, D_FF, D_MODEL), 0.5 * D_FF ** -0.5)
    return {'x_prompt': x_prompt, 'x_sample': x_sample, 'mem_prompt': mem_prompt,
            'cache_k': cache_k, 'cache_v': cache_v, 'cache_idx': cache_idx,
            'cache_mem_k': cache_mem_k, 'cache_mem_v': cache_mem_v,
            'state_delta': state_delta, 'state_conv': state_conv, 'page_table': page_table,
            'norm_mix': norm_mix, 'norm_ffn': norm_ffn, 'attn_q_norm': attn_q_norm,
            'attn_k_norm': attn_k_norm, 'mem_q_norm': mem_q_norm, 'mem_k_norm': mem_k_norm,
            'w_in_moba': w_in_moba, 'w_in_dsa': w_in_dsa, 'w_in_gdn': w_in_gdn,
            'gdn_conv': gdn_conv, 'gdn_a_log': gdn_a_log, 'gdn_dt_bias': gdn_dt_bias,
            'gdn_out_norm': gdn_out_norm, 'w_mem_kv': w_mem_kv, 'w_out': w_out,
            'w_up': w_up, 'w_down': w_down}


def reference(x_prompt, x_sample, mem_prompt, cache_k, cache_v, cache_idx, cache_mem_k, cache_mem_v,
              state_delta, state_conv, page_table, norm_mix, norm_ffn, attn_q_norm, attn_k_norm,
              mem_q_norm, mem_k_norm, w_in_moba, w_in_dsa, w_in_gdn, gdn_conv, gdn_a_log, gdn_dt_bias,
              gdn_out_norm, w_mem_kv, w_out, w_up, w_down):
    W = {'norm_mix': norm_mix, 'norm_ffn': norm_ffn, 'attn_q_norm': attn_q_norm,
         'attn_k_norm': attn_k_norm, 'mem_q_norm': mem_q_norm, 'w_in_moba': w_in_moba,
         'w_in_dsa': w_in_dsa, 'w_in_gdn': w_in_gdn, 'gdn_conv': gdn_conv, 'gdn_a_log': gdn_a_log,
         'gdn_dt_bias': gdn_dt_bias, 'gdn_out_norm': gdn_out_norm, 'w_out': w_out,
         'w_up': w_up, 'w_down': w_down}
    pos_p = jnp.arange(SEQ, dtype=jnp.int32)
    pos_s = PAST_LEN + jnp.arange(DEC_SEQ, dtype=jnp.int32)
    mem_k_p, mem_v_p = memory_kv(mem_prompt, w_mem_kv, mem_k_norm)
    y_prompt, k_p, v_p, idx_p, delta_p, conv_p = trunk(x_prompt, pos_p, mem_k_p, mem_v_p, None, W)
    past = {'cache_k': cache_k, 'cache_v': cache_v, 'cache_idx': cache_idx,
            'state_delta': state_delta, 'state_conv': state_conv, 'page_table': page_table}
    y_sample, k_s, v_s, idx_s, delta_s, conv_s = trunk(x_sample, pos_s, cache_mem_k, cache_mem_v, past, W)
    return (y_prompt, y_sample, k_p, v_p, idx_p, mem_k_p, mem_v_p, delta_p, conv_p,
            k_s, v_s, idx_s, delta_s, conv_s)
```

```python
import functools
import math

import jax
import jax.numpy as jnp
from jax import lax
from jax.experimental import pallas as pl
from jax.experimental.pallas import tpu as pltpu

F32 = jnp.float32
BF16 = jnp.bfloat16
I32 = jnp.int32
HI = lax.Precision.HIGHEST

D_MODEL = 1024
HEAD_DIM = 64
N_HEADS = 12
ATT_W = N_HEADS * HEAD_DIM
MEM_HEADS = 4
MEM_W = MEM_HEADS * HEAD_DIM
ROT_DIM = HEAD_DIM // 4
ROPE_THETA = 500000.0
ATT_SCALE = HEAD_DIM ** -0.5
MOBA_BLOCK = 256
MOBA_TOPK = 3
IDX_HEADS = 8
IDX_DIM = 32
IDX_ROT = IDX_DIM // 4
DSA_TOPK = 256
GDN_HEADS = 6
GDN_DK = 128
GDN_W = GDN_HEADS * GDN_DK
CONV_W = 4
CONV_CH = 3 * GDN_W
GDN_CHUNK = 64
D_FF = 4 * D_MODEL
EPS = 1e-6
NEG = -1e30
PAGE_SIZE = 128
N_MIXERS = 3

LANES = 128
SUBLANES = 8
ROW_TILE = 256
SAMPLE_ROWS = 8
VMEM_LIMIT = 56 * 1024 * 1024
INT_MIN = -2 ** 31
KEYS_PER_ROW = LANES // IDX_DIM
PAGES_PER_STEP = 4

MOBA_C = 3 * ATT_W + MEM_W
DSA_C = 3 * ATT_W + 2 * MEM_W + LANES
GDN_C = 4 * GDN_W + MEM_W + LANES


def _cparams(sem):
    return pltpu.CompilerParams(dimension_semantics=sem, vmem_limit_bytes=VMEM_LIMIT)


def _dot_nt(a, b, precision=None):
    return lax.dot_general(a, b, (((1,), (1,)), ((), ())), precision=precision,
                           preferred_element_type=F32)


def _dot3(a, b, dims):
    a_hi = a.astype(BF16)
    b_hi = b.astype(BF16)
    a_lo = (a - a_hi.astype(F32)).astype(BF16)
    b_lo = (b - b_hi.astype(F32)).astype(BF16)
    dg = lambda x, y: lax.dot_general(x, y, dims, preferred_element_type=F32)
    return dg(a_hi, b_hi) + dg(a_hi, b_lo) + dg(a_lo, b_hi)


def _head_block_ones(width):
    r = lax.broadcasted_iota(I32, (LANES, LANES), 0) // width
    c = lax.broadcasted_iota(I32, (LANES, LANES), 1) // width
    return jnp.where(r == c, 1.0, 0.0).astype(BF16)


def _group_sum(x2, gmat):
    hi = x2.astype(BF16)
    lo = (x2 - hi.astype(F32)).astype(BF16)
    return (jnp.dot(hi, gmat, preferred_element_type=F32)
            + jnp.dot(lo, gmat, preferred_element_type=F32))


def _head_rms(x, gain, gmat, width):
    ms = _group_sum(x * x, gmat) * (1.0 / width)
    return x * lax.rsqrt(ms + EPS) * gain


def _rope(y, c, sa, sb, half):
    return y * c + pltpu.roll(y, LANES - half, 1) * sa + pltpu.roll(y, half, 1) * sb


def _norm_matmul_kernel(x_ref, g_ref, w_ref, o_ref):
    x = x_ref[...]
    ms = jnp.mean(x * x, axis=-1, keepdims=True)
    h = (x * lax.rsqrt(ms + EPS) * g_ref[...]).astype(BF16)
    o_ref[...] = jnp.dot(h, w_ref[...], preferred_element_type=F32)


def norm_matmul(x, g, w):
    r, d = x.shape
    c = w.shape[1]
    tm = ROW_TILE
    return pl.pallas_call(
        _norm_matmul_kernel,
        grid=(r // tm,),
        in_specs=[pl.BlockSpec((tm, d), lambda i: (i, 0)),
                  pl.BlockSpec((1, d), lambda i: (0, 0)),
                  pl.BlockSpec((d, c), lambda i: (0, 0))],
        out_specs=pl.BlockSpec((tm, c), lambda i: (i, 0)),
        out_shape=jax.ShapeDtypeStruct((r, c), F32),
        compiler_params=_cparams(("parallel",)),
        name="norm_matmul",
    )(x, g.reshape(1, d), w)


def _matmul_kernel(x_ref, w_ref, o_ref):
    o_ref[...] = jnp.dot(x_ref[...].astype(BF16), w_ref[...], preferred_element_type=F32)


def layer_matmul(x, w):
    r, d = x.shape
    nl, _, c = w.shape
    return pl.pallas_call(
        _matmul_kernel,
        grid=(nl,),
        in_specs=[pl.BlockSpec((r, d), lambda l: (0, 0)),
                  pl.BlockSpec((None, d, c), lambda l: (l, 0, 0))],
        out_specs=pl.BlockSpec((None, r, c), lambda l: (l, 0, 0)),
        out_shape=jax.ShapeDtypeStruct((nl, r, c), F32),
        compiler_params=_cparams(("parallel",)),
        name="mem_kv_matmul",
    )(x, w)


def _mem_k_norm_kernel(kv_ref, g_ref, o_ref):
    gmat = _head_block_ones(HEAD_DIM)
    g = g_ref[...]
    for p in range(MEM_W // LANES):
        x = kv_ref[:, p * LANES:(p + 1) * LANES]
        o_ref[:, p * LANES:(p + 1) * LANES] = _head_rms(x, g, gmat, HEAD_DIM)


def mem_key_norm(kv, gains):
    nl, r, _ = kv.shape
    return pl.pallas_call(
        _mem_k_norm_kernel,
        grid=(nl,),
        in_specs=[pl.BlockSpec((None, r, MEM_W), lambda l: (l, 0, 0)),
                  pl.BlockSpec((None, 1, LANES), lambda l: (l, 0, 0))],
        out_specs=pl.BlockSpec((None, r, MEM_W), lambda l: (l, 0, 0)),
        out_shape=jax.ShapeDtypeStruct((nl, r, MEM_W), F32),
        compiler_params=_cparams(("parallel",)),
        name="mem_k_norm",
    )(kv, gains)


def _qk_post_kernel(is_dsa, *refs):
    if is_dsa:
        (q_ref, k_ref, qi_ref, kw_ref, c_ref, sa_ref, sb_ref, ci_ref, sai_ref, sbi_ref,
         qg_ref, kg_ref, qo_ref, ko_ref, km_ref, qio_ref, kit_ref) = refs
    else:
        (q_ref, k_ref, c_ref, sa_ref, sb_ref, qg_ref, kg_ref, qo_ref, ko_ref, km_ref) = refs
    gmat = _head_block_ones(HEAD_DIM)
    c, sa, sb = c_ref[...], sa_ref[...], sb_ref[...]
    qg, kg = qg_ref[...], kg_ref[...]
    half = ROT_DIM // 2
    for p in range(ATT_W // LANES):
        sl = slice(p * LANES, (p + 1) * LANES)
        qn = _rope(_head_rms(q_ref[:, sl], qg, gmat, HEAD_DIM), c, sa, sb, half)
        qo_ref[:, sl] = qn * ATT_SCALE
        kn = _rope(_head_rms(k_ref[:, sl], kg, gmat, HEAD_DIM), c, sa, sb, half)
        ko_ref[:, sl] = kn
        km_ref[:, sl] = jnp.sum(kn, axis=0, keepdims=True) * (1.0 / ROW_TILE)
    if is_dsa:
        ci, sai, sbi = ci_ref[...], sai_ref[...], sbi_ref[...]
        ihalf = IDX_ROT // 2
        for p in range(IDX_HEADS * IDX_DIM // LANES):
            sl = slice(p * LANES, (p + 1) * LANES)
            qio_ref[:, sl] = _rope(qi_ref[:, sl], ci, sai, sbi, ihalf)
        kw = kw_ref[...]
        lane = lax.broadcasted_iota(I32, kw.shape, 1)
        kz = jnp.where(lane < IDX_DIM, kw, 0.0)
        kt = kz
        for s in range(1, LANES // IDX_DIM):
            kt = kt + pltpu.roll(kz, s * IDX_DIM, 1)
        kit_ref[...] = _rope(kt, ci, sai, sbi, ihalf)


def qk_post(proj, is_dsa, tabs, itabs, q_gain, k_gain):
    r = proj.shape[0]
    tm = ROW_TILE
    row = lambda cb: (lambda i: (i, cb))
    tab_spec = pl.BlockSpec((tm, LANES), lambda i: (i, 0))
    gain_spec = pl.BlockSpec((1, LANES), lambda i: (0, 0))
    in_specs = [pl.BlockSpec((tm, ATT_W), row(0)), pl.BlockSpec((tm, ATT_W), row(1))]
    args = [proj, proj]
    if is_dsa:
        in_specs += [pl.BlockSpec((tm, MEM_W), row(3 * ATT_W // MEM_W)),
                     pl.BlockSpec((tm, LANES), row((3 * ATT_W + 2 * MEM_W) // LANES))]
        args += [proj, proj]
    in_specs += [tab_spec] * 3
    args += list(tabs)
    if is_dsa:
        in_specs += [tab_spec] * 3
        args += list(itabs)
    in_specs += [gain_spec, gain_spec]
    args += [q_gain, k_gain]
    out_specs = [pl.BlockSpec((tm, ATT_W), row(0)), pl.BlockSpec((tm, ATT_W), row(0)),
                 pl.BlockSpec((None, 1, ATT_W), lambda i: (i, 0, 0))]
    out_shape = [jax.ShapeDtypeStruct((r, ATT_W), F32), jax.ShapeDtypeStruct((r, ATT_W), F32),
                 jax.ShapeDtypeStruct((r // tm, 1, ATT_W), F32)]
    if is_dsa:
        out_specs += [pl.BlockSpec((tm, MEM_W), row(0)), pl.BlockSpec((tm, LANES), row(0))]
        out_shape += [jax.ShapeDtypeStruct((r, MEM_W), F32), jax.ShapeDtypeStruct((r, LANES), F32)]
    return pl.pallas_call(
        functools.partial(_qk_post_kernel, is_dsa),
        grid=(r // tm,),
        in_specs=in_specs, out_specs=out_specs, out_shape=out_shape,
        compiler_params=_cparams(("parallel",)),
        name="qk_post_dsa" if is_dsa else "qk_post",
    )(*args)


def _post_kernel(x_ref, o_ref, om_ref, wa_ref, wb_ref, g_ref, wu_ref, wd_ref, y_ref, h_sc):
    f = pl.program_id(1)

    @pl.when(f == 0)
    def _():
        x1 = (x_ref[...]
              + jnp.dot(o_ref[...].astype(BF16), wa_ref[...], preferred_element_type=F32)
              + jnp.dot(om_ref[...].astype(BF16), wb_ref[...], preferred_element_type=F32))
        ms = jnp.mean(x1 * x1, axis=-1, keepdims=True)
        h_sc[...] = (x1 * lax.rsqrt(ms + EPS) * g_ref[...]).astype(BF16)
        y_ref[...] = x1

    up = jnp.dot(h_sc[...], wu_ref[...], preferred_element_type=F32)
    act = jnp.square(jnp.maximum(up, 0.0)).astype(BF16)
    y_ref[...] += jnp.dot(act, wd_ref[...], preferred_element_type=F32)


def post_block(x, o, om, wa, wb, g, wu, wd, tm, tf):
    r, d = x.shape
    return pl.pallas_call(
        _post_kernel,
        grid=(r // tm, D_FF // tf),
        in_specs=[pl.BlockSpec((tm, d), lambda i, f: (i, 0)),
                  pl.BlockSpec((tm, ATT_W), lambda i, f: (i, 0)),
                  pl.BlockSpec((tm, MEM_W), lambda i, f: (i, 0)),
                  pl.BlockSpec((ATT_W, d), lambda i, f: (0, 0)),
                  pl.BlockSpec((MEM_W, d), lambda i, f: (0, 0)),
                  pl.BlockSpec((1, d), lambda i, f: (0, 0)),
                  pl.BlockSpec((d, tf), lambda i, f: (0, f)),
                  pl.BlockSpec((tf, d), lambda i, f: (f, 0))],
        out_specs=pl.BlockSpec((tm, d), lambda i, f: (i, 0)),
        out_shape=jax.ShapeDtypeStruct((r, d), F32),
        scratch_shapes=[pltpu.VMEM((tm, d), BF16)],
        compiler_params=_cparams(("parallel", "arbitrary")),
        name="out_proj_ffn",
    )(x, o, om, wa, wb, g.reshape(1, d), wu, wd)


def _flash_update(s, v_bf, m_ref, l_ref, acc_ref):
    m_old = m_ref[...]
    m_new = jnp.maximum(m_old, jnp.max(s, axis=1, keepdims=True))
    alpha = jnp.exp(m_old - m_new)
    p = jnp.exp(s - m_new)
    l_ref[...] = alpha * l_ref[...] + jnp.sum(p, axis=1, keepdims=True)
    acc_ref[...] = alpha * acc_ref[...] + jnp.dot(p.astype(BF16), v_bf, preferred_element_type=F32)
    m_ref[...] = m_new


def _topk_block_bias(gate, n_past, n_top):
    nbk = gate.shape[1]
    col = lax.broadcasted_iota(I32, gate.shape, 1)
    past = col < n_past
    g = jnp.where(past, gate, NEG)
    sel = jnp.zeros(gate.shape, F32)
    for _ in range(n_top):
        mx = jnp.max(g, axis=1, keepdims=True)
        idx = jnp.min(jnp.where(g == mx, col, nbk), axis=1, keepdims=True)
        pick = col == idx
        sel = jnp.where(pick, 1.0, sel)
        g = jnp.where(pick, -jnp.inf, g)
    return jnp.where(past, jnp.where(sel > 0.5, 0.0, NEG), NEG)


def _topk_block_bias_t(gate, n_past, n_top):
    nbk = gate.shape[0]
    row = lax.broadcasted_iota(I32, gate.shape, 0)
    past = row < n_past
    g = jnp.where(past, gate, NEG)
    sel = jnp.zeros(gate.shape, F32)
    for _ in range(n_top):
        mx = jnp.max(g, axis=0, keepdims=True)
        idx = jnp.min(jnp.where(g == mx, row, nbk), axis=0, keepdims=True)
        pick = row == idx
        sel = jnp.where(pick, 1.0, sel)
        g = jnp.where(pick, -jnp.inf, g)
    return jnp.where(past, jnp.where(sel > 0.5, 0.0, NEG), NEG)


def _prompt_attn_kernel(is_dsa, nb, n_top, q_ref, k_ref, v_ref, aux_ref, o_ref,
                        kbf_sc, vt_sc, m_sc, l_sc, acc_sc, bias_sc):
    blk = MOBA_BLOCK
    i = pl.program_id(2)

    @pl.when(i == 0)
    def _():
        kbf_sc[...] = k_ref[...].astype(BF16)
        for c in range(nb):
            vt_sc[c] = jnp.transpose(v_ref[c * blk:(c + 1) * blk, :]).astype(BF16)

    q = q_ref[...]
    lane = lax.broadcasted_iota(I32, q.shape, 1)
    qcat = jnp.concatenate([jnp.where(lane // HEAD_DIM == hh, q, 0.0) for hh in range(2)], axis=0)
    qcat_bf = qcat.astype(BF16)
    m_sc[...] = jnp.full(m_sc.shape, NEG, F32)
    l_sc[...] = jnp.zeros(l_sc.shape, F32)
    acc_sc[...] = jnp.zeros(acc_sc.shape, F32)

    def blocks(js, biases):
        sts = [_dot_nt(kbf_sc[pl.ds(pl.multiple_of(j * blk, blk), blk), :], qcat_bf) + b
               for j, b in zip(js, biases)]
        m_old = m_sc[...]
        m_new = m_old
        for st in sts:
            m_new = jnp.maximum(m_new, jnp.max(st, axis=0, keepdims=True))
        alpha = jnp.exp(m_old - m_new)
        ps = [jnp.exp(st - m_new) for st in sts]
        l_new = alpha * l_sc[...]
        acc = alpha * acc_sc[...]
        for j, p in zip(js, ps):
            l_new = l_new + jnp.sum(p, axis=0, keepdims=True)
            acc = acc + jnp.dot(vt_sc[j], p.astype(BF16), preferred_element_type=F32)
        l_sc[...] = l_new
        acc_sc[...] = acc
        m_sc[...] = m_new

    if is_dsa:
        def bias_of(j):
            mb = aux_ref[j].astype(F32)
            return jnp.concatenate([mb, mb], axis=1)

        count = i + 1
    else:
        gate = _dot_nt(aux_ref[...], qcat, HI)
        bias_sc[...] = _topk_block_bias_t(gate, i, n_top)
        kr = lax.broadcasted_iota(I32, (blk, 2 * blk), 0)
        qc = lax.broadcasted_iota(I32, (blk, 2 * blk), 1) % blk
        blocks([i], [jnp.where(kr <= qc, 0.0, NEG)])

        def bias_of(j):
            return bias_sc[pl.ds(j, 1), :]

        count = i

    def pair(jj, carry):
        blocks([2 * jj, 2 * jj + 1], [bias_of(2 * jj), bias_of(2 * jj + 1)])
        return carry

    lax.fori_loop(0, count // 2, pair, 0)

    @pl.when(count % 2 == 1)
    def _():
        blocks([count - 1], [bias_of(count - 1)])

    res = acc_sc[...] / l_sc[...]
    drow = lax.broadcasted_iota(I32, (LANES, blk), 0)
    o_t = jnp.where(drow // HEAD_DIM == 0, res[:, :blk], res[:, blk:])
    o_ref[...] = jnp.transpose(o_t)


def prompt_attention(q_all, k_all, proj, aux, is_dsa, n, t):
    blk = MOBA_BLOCK
    nb = t // blk
    npair = ATT_W // LANES
    n_top = min(MOBA_TOPK, nb)
    vcol = 2 * ATT_W // LANES
    if is_dsa:
        aux_spec = pl.BlockSpec((None, None, nb, blk, blk), lambda b, p, i: (b, i, 0, 0, 0))
    else:
        aux_spec = pl.BlockSpec((nb, LANES), lambda b, p, i: (b, p))
    return pl.pallas_call(
        functools.partial(_prompt_attn_kernel, is_dsa, nb, n_top),
        grid=(n, npair, nb),
        in_specs=[pl.BlockSpec((blk, LANES), lambda b, p, i: (b * nb + i, p)),
                  pl.BlockSpec((t, LANES), lambda b, p, i: (b, p)),
                  pl.BlockSpec((t, LANES), lambda b, p, i: (b, vcol + p)),
                  aux_spec],
        out_specs=pl.BlockSpec((blk, LANES), lambda b, p, i: (b * nb + i, p)),
        out_shape=jax.ShapeDtypeStruct((n * t, ATT_W), F32),
        scratch_shapes=[pltpu.VMEM((t, LANES), BF16), pltpu.VMEM((nb, LANES, blk), BF16),
                        pltpu.VMEM((1, 2 * blk), F32), pltpu.VMEM((1, 2 * blk), F32),
                        pltpu.VMEM((LANES, 2 * blk), F32), pltpu.VMEM((nb, 2 * blk), F32)],
        compiler_params=_cparams(("parallel", "parallel", "arbitrary")),
        name="prompt_attn_dsa" if is_dsa else "prompt_attn_moba",
    )(q_all, k_all, proj, aux)


def _key_to_float(k):
    b = jnp.where(k < 0, k ^ jnp.int32(0x7FFFFFFF), k)
    return lax.bitcast_convert_type(b, F32)


def _select_topk(count_ge, count_tie_lt, k_sel, shape, idx_bits):
    def vbit(b, t):
        cand = t + lax.shift_left(jnp.int32(1), 31 - b)
        return jnp.where(count_ge(_key_to_float(cand)) >= k_sel, cand, t)

    vkey = lax.fori_loop(0, 32, vbit, jnp.full(shape, INT_MIN, I32))
    v = _key_to_float(vkey)
    need = k_sel - count_ge(_key_to_float(vkey + 1))

    def cbit(b, c):
        cand = c + lax.shift_left(jnp.int32(1), idx_bits - 1 - b)
        return jnp.where(count_tie_lt(v, cand) <= need - 1, cand, c)

    c = lax.fori_loop(0, idx_bits, cbit, jnp.zeros(shape, I32))
    return v, c


def _dsa_select_kernel(nb, k_sel, idx_bits, qi_ref, kit_ref, w_ref, o_ref, sc_sc):
    blk = MOBA_BLOCK
    i = pl.program_id(1)
    w_t = jnp.transpose(w_ref[...])
    lane = lax.broadcasted_iota(I32, (blk, LANES), 1)
    qm, wrow = [], []
    for h in range(IDX_HEADS):
        seg = qi_ref[:, (h // 4) * LANES:(h // 4 + 1) * LANES]
        qm.append(jnp.where(lane // IDX_DIM == h % 4, seg, 0.0).astype(BF16))
        wrow.append(w_t[IDX_DIM + h:IDX_DIM + h + 1, :] * (1.0 / 16.0))
    kr = lax.broadcasted_iota(I32, (blk, blk), 0)
    qpos = i * blk + lax.broadcasted_iota(I32, (blk, blk), 1)

    def score_block(c, carry):
        off = pl.multiple_of(c * blk, blk)
        kt = kit_ref[pl.ds(off, blk), :].astype(BF16)
        acc = jnp.zeros((blk, blk), F32)
        for h in range(IDX_HEADS):
            acc = acc + jnp.maximum(_dot_nt(kt, qm[h]), 0.0) * wrow[h]
        sc_sc[c] = jnp.where(c * blk + kr <= qpos, acc + 0.0, -jnp.inf)
        return carry

    lax.fori_loop(0, i + 1, score_block, 0)

    def fold(x):
        return jnp.sum(x.reshape(blk // SUBLANES, SUBLANES, blk), axis=0)

    def count_ge(cand):
        def body(c, acc):
            return acc + fold(jnp.where(sc_sc[c] >= cand, 1, 0))

        acc = lax.fori_loop(0, i + 1, body, jnp.zeros((SUBLANES, blk), I32))
        return jnp.sum(acc, axis=0, keepdims=True)

    def count_tie_lt(v, cut):
        def body(c, acc):
            hit = (sc_sc[c] == v) & (c * blk + kr < cut)
            return acc + fold(jnp.where(hit, 1, 0))

        acc = lax.fori_loop(0, i + 1, body, jnp.zeros((SUBLANES, blk), I32))
        return jnp.sum(acc, axis=0, keepdims=True)

    v, cut = _select_topk(count_ge, count_tie_lt, k_sel, (1, blk), idx_bits)
    take_all = i * blk + lax.broadcasted_iota(I32, (1, blk), 1) + 1 <= k_sel

    def emit(c, carry):
        s = sc_sc[c]
        kpos = c * blk + kr
        bias = jnp.where(s > v, 0.0, jnp.where((s == v) & (kpos <= cut), 0.0, NEG))
        bias = jnp.where(take_all, 0.0, bias)
        o_ref[c] = jnp.where(kpos <= qpos, bias, NEG).astype(BF16)
        return carry

    lax.fori_loop(0, i + 1, emit, 0)

    def fill(c, carry):
        o_ref[c] = jnp.full((blk, blk), NEG, BF16)
        return carry

    lax.fori_loop(i + 1, nb, fill, 0)


def dsa_select_prompt(qi, kit, proj, n, t):
    blk = MOBA_BLOCK
    nb = t // blk
    k_sel = min(DSA_TOPK, t // 4)
    idx_bits = max(1, (t - 1).bit_length())
    wcol = (3 * ATT_W + 2 * MEM_W) // LANES
    return pl.pallas_call(
        functools.partial(_dsa_select_kernel, nb, k_sel, idx_bits),
        grid=(n, nb),
        in_specs=[pl.BlockSpec((blk, MEM_W), lambda b, i: (b * nb + i, 0)),
                  pl.BlockSpec((t, LANES), lambda b, i: (b, 0)),
                  pl.BlockSpec((blk, LANES), lambda b, i: (b * nb + i, wcol))],
        out_specs=pl.BlockSpec((None, None, nb, blk, blk), lambda b, i: (b, i, 0, 0, 0)),
        out_shape=jax.ShapeDtypeStruct((n, nb, nb, blk, blk), BF16),
        scratch_shapes=[pltpu.VMEM((nb, blk, blk), F32)],
        compiler_params=_cparams(("parallel", "arbitrary")),
        name="dsa_select_prompt",
    )(qi, kit, proj)


def _mem_attn_kernel(q_ref, g_ref, mk_ref, mv_ref, o_ref):
    gmat = _head_block_ones(HEAD_DIM)
    g = g_ref[...]
    for p in range(MEM_W // LANES):
        sl = slice(p * LANES, (p + 1) * LANES)
        q = _head_rms(q_ref[:, sl], g, gmat, HEAD_DIM) * ATT_SCALE
        lane = lax.broadcasted_iota(I32, q.shape, 1)
        kb = mk_ref[:, sl].astype(BF16)
        vb = mv_ref[:, sl].astype(BF16)
        outs = []
        for hh in range(2):
            qh = jnp.where(lane // HEAD_DIM == hh, q, 0.0).astype(BF16)
            s = _dot_nt(qh, kb)
            pexp = jnp.exp(s - jnp.max(s, axis=1, keepdims=True))
            den = jnp.sum(pexp, axis=1, keepdims=True)
            outs.append(jnp.dot(pexp.astype(BF16), vb, preferred_element_type=F32) / den)
        o_ref[:, sl] = jnp.where(lane // HEAD_DIM == 0, outs[0], outs[1])


def mem_attention(proj, qcol, gain, mk, mv, row0, n, t, tq):
    nq = t // tq
    rb0 = row0 // tq
    ml = mk.shape[1]
    return pl.pallas_call(
        _mem_attn_kernel,
        grid=(n, nq),
        in_specs=[pl.BlockSpec((tq, MEM_W), lambda b, i: (rb0 + b * nq + i, qcol)),
                  pl.BlockSpec((1, LANES), lambda b, i: (0, 0)),
                  pl.BlockSpec((None, ml, MEM_W), lambda b, i: (b, 0, 0)),
                  pl.BlockSpec((None, ml, MEM_W), lambda b, i: (b, 0, 0))],
        out_specs=pl.BlockSpec((tq, MEM_W), lambda b, i: (b * nq + i, 0)),
        out_shape=jax.ShapeDtypeStruct((n * t, MEM_W), F32),
        compiler_params=_cparams(("parallel", "parallel")),
        name="mem_attn",
    )(proj, gain, mk, mv)


def _tile_rows(x, reps):
    return jnp.concatenate([x] * reps, axis=0)


def _block_diag_queries(q8):
    qt = _tile_rows(q8, N_HEADS)
    row = lax.broadcasted_iota(I32, qt.shape, 0)
    lane = lax.broadcasted_iota(I32, qt.shape, 1)
    return jnp.where(lane // HEAD_DIM == row // SAMPLE_ROWS, qt, 0.0)


def _sample_attn_kernel(nsteps, nbp, n_top, use_gate, *refs):
    npp = PAGES_PER_STEP
    pt_ref = refs[0]
    k_refs, v_refs = refs[1:1 + npp], refs[1 + npp:1 + 2 * npp]
    (q_ref, kn_ref, vn_ref, km_ref, nbias_ref, o_ref,
     qbd_sc, kmean_sc, m_sc, l_sc, acc_sc) = refs[1 + 2 * npp:]
    j = pl.program_id(1)
    ppb = MOBA_BLOCK // PAGE_SIZE

    @pl.when(j == 0)
    def _():
        qbd_sc[...] = _block_diag_queries(q_ref[...])
        kmean_sc[...] = jnp.zeros(kmean_sc.shape, F32)

    qbd = qbd_sc[...]
    qbd_bf = qbd.astype(BF16)
    kmask = km_ref[...]
    ones = jnp.ones((PAGE_SIZE, LANES), BF16)
    lane = lax.broadcasted_iota(I32, (ATT_W, LANES), 1)
    for blk in range(npp // ppb):
        bi = (npp // ppb) * j + blk
        scores, ksum = [], jnp.zeros((ATT_W, LANES), F32)
        for pg in range(ppb):
            kf = k_refs[blk * ppb + pg][...]
            khi = kf.astype(BF16)
            scores.append(jnp.dot(qbd_bf, khi, preferred_element_type=F32))
            if use_gate:
                klo = (kf - khi.astype(F32)).astype(BF16)
                ksum = (ksum + jnp.dot(khi, ones, preferred_element_type=F32)
                        + jnp.dot(klo, ones, preferred_element_type=F32))
        if use_gate:
            kmean_sc[...] += jnp.where(lane == bi, ksum * (1.0 / MOBA_BLOCK), 0.0)
        s = jnp.concatenate(scores, axis=1)
        s = s + _tile_rows(kmask[:, blk * MOBA_BLOCK:(blk + 1) * MOBA_BLOCK], N_HEADS)
        m = jnp.max(s, axis=1, keepdims=True)
        p = jnp.exp(s - m)
        pb = p.astype(BF16)
        acc = jnp.zeros((N_HEADS * SAMPLE_ROWS, ATT_W), F32)
        for pg in range(ppb):
            vb = v_refs[blk * ppb + pg][...].astype(BF16)
            acc = acc + _dot_nt(pb[:, pg * PAGE_SIZE:(pg + 1) * PAGE_SIZE], vb)
        m_sc[bi] = m
        l_sc[bi] = jnp.sum(p, axis=1, keepdims=True)
        acc_sc[bi] = acc

    @pl.when(j == nsteps - 1)
    def _():
        pad = jnp.zeros((PAGE_SIZE - SAMPLE_ROWS, ATT_W), F32)
        kn = jnp.concatenate([kn_ref[...], pad], axis=0).astype(BF16)
        vn = jnp.concatenate([vn_ref[...], pad], axis=0).astype(BF16)
        s = _dot_nt(qbd_bf, kn) + _tile_rows(nbias_ref[...], N_HEADS)
        m_n = jnp.max(s, axis=1, keepdims=True)
        p = jnp.exp(s - m_n)
        l_n = jnp.sum(p, axis=1, keepdims=True)
        acc_n = jnp.dot(p.astype(BF16), vn, preferred_element_type=F32)
        if use_gate:
            gate = jnp.dot(qbd, kmean_sc[...], precision=HI, preferred_element_type=F32)
            bias = _topk_block_bias(gate, nbp, n_top)
            mb = [m_sc[b] + bias[:, b:b + 1] for b in range(nbp)]
        else:
            mb = [m_sc[b] for b in range(nbp)]
        m_all = m_n
        for b in range(nbp):
            m_all = jnp.maximum(m_all, mb[b])
        w_n = jnp.exp(m_n - m_all)
        den = w_n * l_n
        num = w_n * acc_n
        for b in range(nbp):
            w_b = jnp.exp(mb[b] - m_all)
            den = den + w_b * l_sc[b]
            num = num + w_b * acc_sc[b]
        res = num / den
        olane = lax.broadcasted_iota(I32, (SAMPLE_ROWS, ATT_W), 1)
        out = jnp.zeros((SAMPLE_ROWS, ATT_W), F32)
        for h in range(N_HEADS):
            out = out + jnp.where(olane // HEAD_DIM == h,
                                  res[h * SAMPLE_ROWS:(h + 1) * SAMPLE_ROWS, :], 0.0)
        o_ref[...] = out


def sample_attention(kt_all, vt_all, slot, page_table, q_all, k_all, proj, key_bias, new_bias,
                     use_gate, row0, ns):
    npg = page_table.shape[1]
    npp = PAGES_PER_STEP
    assert npg % npp == 0
    nsteps = npg // npp
    nbp = npg // (MOBA_BLOCK // PAGE_SIZE)
    assert nbp <= LANES
    n_top = min(MOBA_TOPK, nbp + 1)
    rb0 = row0 // SAMPLE_ROWS
    rows = N_HEADS * SAMPLE_ROWS
    page = lambda off: pl.BlockSpec((None, None, ATT_W, PAGE_SIZE),
                                    lambda b, j, pt: (slot, pt[b, npp * j + off], 0, 0))
    rowblk = lambda cb: pl.BlockSpec((SAMPLE_ROWS, ATT_W), lambda b, j, pt: (rb0 + b, cb))
    grid_spec = pltpu.PrefetchScalarGridSpec(
        num_scalar_prefetch=1,
        grid=(ns, nsteps),
        in_specs=[page(off) for off in range(npp)] * 2 + [
            rowblk(0), rowblk(0), rowblk(2),
            pl.BlockSpec((None, SAMPLE_ROWS, npp * PAGE_SIZE), lambda b, j, pt: (b, 0, j)),
            pl.BlockSpec((None, SAMPLE_ROWS, LANES), lambda b, j, pt: (b, 0, 0))],
        out_specs=pl.BlockSpec((SAMPLE_ROWS, ATT_W), lambda b, j, pt: (b, 0)),
        scratch_shapes=[pltpu.VMEM((rows, ATT_W), F32), pltpu.VMEM((ATT_W, LANES), F32),
                        pltpu.VMEM((nbp, rows, 1), F32), pltpu.VMEM((nbp, rows, 1), F32),
                        pltpu.VMEM((nbp, rows, ATT_W), F32)],
    )
    return pl.pallas_call(
        functools.partial(_sample_attn_kernel, nsteps, nbp, n_top, use_gate),
        grid_spec=grid_spec,
        out_shape=jax.ShapeDtypeStruct((ns * SAMPLE_ROWS, ATT_W), F32),
        compiler_params=_cparams(("parallel", "arbitrary")),
        name="sample_attn_moba" if use_gate else "sample_attn_dsa",
    )(page_table, *([kt_all] * npp), *([vt_all] * npp), q_all, k_all, proj, key_bias, new_bias)


def _dsa_select_sample_kernel(npg, ts, k_sel, idx_bits, pt_ref, cidx_ref, qi_ref, kit_ref, w_ref,
                              km_ref, nb_ref, kbuf, sem):
    b = pl.program_id(0)
    past = npg * PAGE_SIZE
    kpr = PAGE_SIZE // KEYS_PER_ROW
    nrow = past // KEYS_PER_ROW

    def page_copy(p):
        return pltpu.make_async_copy(cidx_ref.at[pt_ref[b, p]], kbuf.at[pl.ds(p * kpr, kpr)], sem)

    for p in range(npg):
        page_copy(p).start()
    for p in range(npg):
        page_copy(p).wait()

    lane = lax.broadcasted_iota(I32, (SAMPLE_ROWS, LANES), 1)
    wblk = w_ref[...]
    kpast = kbuf[...].astype(BF16)
    knew = jnp.where(lane < IDX_DIM, kit_ref[...], 0.0)
    knew = jnp.concatenate([knew, jnp.zeros((LANES - SAMPLE_ROWS, LANES), F32)], axis=0).astype(BF16)
    isc_p = [jnp.zeros((SAMPLE_ROWS, nrow), F32) for _ in range(KEYS_PER_ROW)]
    isc_n = jnp.zeros((SAMPLE_ROWS, LANES), F32)
    for h in range(IDX_HEADS):
        seg = qi_ref[:, (h // 4) * LANES:(h // 4 + 1) * LANES]
        w_h = jnp.sum(jnp.where(lane == IDX_DIM + h, wblk, 0.0), axis=1, keepdims=True) * (1.0 / 16.0)
        for c in range(KEYS_PER_ROW):
            shift = ((c - h % 4) * IDX_DIM) % LANES
            qs = pltpu.roll(seg, shift, 1) if shift else seg
            qmh = jnp.where(lane // IDX_DIM == c, qs, 0.0).astype(BF16)
            isc_p[c] = isc_p[c] + jnp.maximum(_dot_nt(qmh, kpast), 0.0) * w_h
            if c == 0:
                isc_n = isc_n + jnp.maximum(_dot_nt(qmh, knew), 0.0) * w_h
    key_p = [x + 0.0 for x in isc_p]
    trow = lax.broadcasted_iota(I32, (SAMPLE_ROWS, LANES), 0)
    valid_n = (lane <= trow) & (lane < ts)
    key_n = jnp.where(valid_n, isc_n + 0.0, -jnp.inf)
    rid = lax.broadcasted_iota(I32, (SAMPLE_ROWS, nrow), 1)
    idx_p = [KEYS_PER_ROW * rid + c for c in range(KEYS_PER_ROW)]
    idx_n = past + lane
    cnt = lambda m: jnp.sum(jnp.where(m, 1, 0), axis=1, keepdims=True)

    def count_ge(cand):
        return sum([cnt(k >= cand) for k in key_p], cnt(key_n >= cand))

    def count_tie_lt(v, cut):
        return sum([cnt((k == v) & (ix < cut)) for k, ix in zip(key_p, idx_p)],
                   cnt((key_n == v) & (idx_n < cut)))

    v, cut = _select_topk(count_ge, count_tie_lt, k_sel, (SAMPLE_ROWS, 1), idx_bits)
    trow1 = lax.broadcasted_iota(I32, (SAMPLE_ROWS, 1), 0)
    take_all = past + jnp.minimum(trow1, ts - 1) + 1 <= k_sel
    for c in range(KEYS_PER_ROW):
        sel = (key_p[c] > v) | ((key_p[c] == v) & (idx_p[c] <= cut))
        km_ref[c] = jnp.where(take_all, 0.0, jnp.where(sel, 0.0, NEG))
    sel_n = (key_n > v) | ((key_n == v) & (idx_n <= cut))
    nb_ref[...] = jnp.where(valid_n, jnp.where(take_all, 0.0, jnp.where(sel_n, 0.0, NEG)), NEG)


def dsa_select_sample(cache_idx_slot, page_table, qi, kit, proj, row0, ns, ts):
    npg = page_table.shape[1]
    past = npg * PAGE_SIZE
    k_sel = min(DSA_TOPK, (past + ts) // 4)
    idx_bits = max(1, (past + LANES - 1).bit_length())
    rb0 = row0 // SAMPLE_ROWS
    wcol = (3 * ATT_W + 2 * MEM_W) // LANES
    nrow = past // KEYS_PER_ROW
    pages = cache_idx_slot.reshape(-1, PAGE_SIZE // KEYS_PER_ROW, LANES)
    grid_spec = pltpu.PrefetchScalarGridSpec(
        num_scalar_prefetch=1,
        grid=(ns,),
        in_specs=[pl.BlockSpec(memory_space=pl.ANY),
                  pl.BlockSpec((SAMPLE_ROWS, MEM_W), lambda b, pt: (rb0 + b, 0)),
                  pl.BlockSpec((SAMPLE_ROWS, LANES), lambda b, pt: (rb0 + b, 0)),
                  pl.BlockSpec((SAMPLE_ROWS, LANES), lambda b, pt: (rb0 + b, wcol))],
        out_specs=[pl.BlockSpec((None, KEYS_PER_ROW, SAMPLE_ROWS, nrow), lambda b, pt: (b, 0, 0, 0)),
                   pl.BlockSpec((None, SAMPLE_ROWS, LANES), lambda b, pt: (b, 0, 0))],
        scratch_shapes=[pltpu.VMEM((nrow, LANES), F32), pltpu.SemaphoreType.DMA(())],
    )
    km, nbias = pl.pallas_call(
        functools.partial(_dsa_select_sample_kernel, npg, ts, k_sel, idx_bits),
        grid_spec=grid_spec,
        out_shape=[jax.ShapeDtypeStruct((ns, KEYS_PER_ROW, SAMPLE_ROWS, nrow), F32),
                   jax.ShapeDtypeStruct((ns, SAMPLE_ROWS, LANES), F32)],
        compiler_params=_cparams(("arbitrary",)),
        name="dsa_select_sample",
    )(page_table, pages, qi, kit, proj)
    return jnp.transpose(km, (0, 2, 3, 1)).reshape(ns, SAMPLE_ROWS, past), nbias


def _gdn_local_kernel(tc, cs, t_valid, q_ref, k_ref, v_ref, ba_ref, cwq_ref, cwk_ref, cwv_ref,
                      cbq_ref, cbk_ref, cbv_ref, hp_ref, u_ref, w_ref, qd_ref, kd_ref, at_ref, xp_sc):
    h = pl.program_id(1)
    c = pl.program_id(2)
    kw = CONV_W - 1

    @pl.when(c == 0)
    def _():
        for seg, cb in enumerate((cbq_ref, cbk_ref, cbv_ref)):
            xp_sc[seg, SUBLANES - kw:SUBLANES, :] = cb[...]

    acts = []
    for seg, (xr, cw) in enumerate(((q_ref, cwq_ref), (k_ref, cwk_ref), (v_ref, cwv_ref))):
        xp_sc[seg, SUBLANES:SUBLANES + tc, :] = xr[...]
        y = xp_sc[seg, SUBLANES - kw:SUBLANES - kw + tc, :] * cw[0:1, :]
        for jj in range(1, CONV_W):
            y = y + xp_sc[seg, SUBLANES - kw + jj:SUBLANES - kw + jj + tc, :] * cw[jj:jj + 1, :]
        acts.append(jax.nn.silu(y))
        xp_sc[seg, SUBLANES - kw:SUBLANES, :] = xp_sc[seg, SUBLANES + tc - kw:SUBLANES + tc, :]
    qa, ka, va = acts
    qn = qa * lax.rsqrt(jnp.sum(qa * qa, axis=-1, keepdims=True) + EPS) * (GDN_DK ** -0.5)
    kn = ka * lax.rsqrt(jnp.sum(ka * ka, axis=-1, keepdims=True) + EPS)

    lane = lax.broadcasted_iota(I32, (tc, LANES), 1)
    ba = ba_ref[...]
    bcol = jnp.sum(jnp.where(lane == h, ba, 0.0), axis=1, keepdims=True)
    acol = jnp.sum(jnp.where(lane == GDN_HEADS + h, ba, 0.0), axis=1, keepdims=True)
    lane1 = lax.broadcasted_iota(I32, (1, LANES), 1)
    hp = hp_ref[...]
    alog = jnp.sum(jnp.where(lane1 == h, hp[0:1, :], 0.0), axis=1, keepdims=True)
    dtb = jnp.sum(jnp.where(lane1 == h, hp[1:2, :], 0.0), axis=1, keepdims=True)
    beta = jax.nn.sigmoid(bcol)
    g = -jnp.exp(alog) * jax.nn.softplus(acol + dtb)
    if t_valid is not None:
        rowid = c * tc + lax.broadcasted_iota(I32, (tc, 1), 0)
        beta = jnp.where(rowid < t_valid, beta, 0.0)
        g = jnp.where(rowid < t_valid, g, 0.0)

    ri = lax.broadcasted_iota(I32, (cs, cs), 0)
    ci = lax.broadcasted_iota(I32, (cs, cs), 1)
    tri = ri >= ci
    stri = ri > ci
    tril = jnp.where(tri, 1.0, 0.0)
    eye = jnp.where(ri == ci, 1.0, 0.0)
    e0 = jnp.where(lax.broadcasted_iota(I32, (cs, LANES), 1) == 0, 1.0, 0.0)
    mm = lambda a, b: _dot3(a, b, (((1,), (0,)), ((), ())))
    mm_nt = lambda a, b: _dot3(a, b, (((1,), (1,)), ((), ())))
    n_sq = max(0, (cs - 1).bit_length() - 1)

    chunks = [slice(ch * cs, (ch + 1) * cs) for ch in range(tc // cs)]
    gcum = [jnp.dot(tril, jnp.broadcast_to(g[rs], (cs, LANES)), precision=HI,
                    preferred_element_type=F32) for rs in chunks]
    grow = [_dot_nt(e0, x, HI) for x in gcum]
    decay = [jnp.where(tri, jnp.exp(jnp.where(tri, gc[:, 0:cs] - gr, 0.0)), 0.0)
             for gc, gr in zip(gcum, grow)]
    kb = [kn[rs] * beta[rs] for rs in chunks]
    pw = [-jnp.where(stri, mm_nt(kbc, kn[rs]) * dc, 0.0) for kbc, rs, dc in zip(kb, chunks, decay)]
    tinv = [eye + p for p in pw]
    for _ in range(n_sq):
        pw = [mm(p, p) for p in pw]
        tinv = [t + mm(t, p) for t, p in zip(tinv, pw)]
    egc = [jnp.exp(x) for x in gcum]
    for ch, rs in enumerate(chunks):
        glast = gcum[ch][cs - 1:cs, :]
        u_ref[rs, :] = mm(tinv[ch], va[rs] * beta[rs])
        w_ref[rs, :] = mm(tinv[ch], kb[ch] * egc[ch])
        qd_ref[rs, :] = qn[rs] * egc[ch]
        kd_ref[rs, :] = kn[rs] * jnp.exp(glast - gcum[ch])
        at_ref[rs, :] = jnp.zeros((cs, LANES), F32)
        at_ref[rs, 0:cs] = jnp.where(tri, mm_nt(qn[rs], kn[rs]) * decay[ch], 0.0)
        at_ref[rs, LANES // 2:LANES] = jnp.broadcast_to(jnp.exp(glast), (cs, LANES))[:, LANES // 2:]


def _gdn_scan_kernel(tc, cs, u_ref, w_ref, qd_ref, kd_ref, at_ref, z_ref, on_ref, s0_ref,
                     o_ref, so_ref, s_sc):
    c = pl.program_id(1)
    nc = pl.num_programs(1)

    @pl.when(c == 0)
    def _():
        s_sc[...] = s0_ref[...]

    mm = lambda a, b: _dot3(a, b, (((1,), (0,)), ((), ())))
    lane = lax.broadcasted_iota(I32, (cs, LANES), 1)
    on = on_ref[...]
    for ch in range(tc // cs):
        rs = slice(ch * cs, (ch + 1) * cs)
        for h in range(GDN_HEADS):
            cl = slice(h * LANES, (h + 1) * LANES)
            at = at_ref[rs, cl]
            decay_tot = jnp.where(lane < LANES // 2, pltpu.roll(at, LANES // 2, 1), at)[0:1, :]
            s_old = s_sc[h]
            v_new = u_ref[rs, cl] - mm(w_ref[rs, cl], s_old)
            o = mm(qd_ref[rs, cl], s_old) + mm(at[:, 0:cs], v_new)
            s_sc[h] = s_old * decay_tot + _dot3(kd_ref[rs, cl], v_new, (((0,), (0,)), ((), ())))
            ms = jnp.mean(o * o, axis=-1, keepdims=True)
            o_ref[rs, cl] = o * lax.rsqrt(ms + EPS) * on * jax.nn.silu(z_ref[rs, cl])

    @pl.when(c == nc - 1)
    def _():
        so_ref[...] = s_sc[...]


def gdn_mix(proj, conv_w, conv_buf, head_params, out_norm, s0, row0, n, t, tc, cs, t_valid):
    nc = t // tc
    rb0 = row0 // tc
    hb = GDN_W // LANES
    bacol = (4 * GDN_W + MEM_W) // LANES
    kw = CONV_W - 1
    assert cs <= LANES // 2
    seg_spec = lambda s: pl.BlockSpec((tc, LANES), lambda b, h, c: (rb0 + b * nc + c, s * hb + h))
    cw_spec = lambda s: pl.BlockSpec((CONV_W, LANES), lambda b, h, c: (0, s * hb + h))
    cb_spec = lambda s: pl.BlockSpec((None, kw, LANES), lambda b, h, c: (b, 0, s * hb + h))
    loc_spec = pl.BlockSpec((tc, LANES), lambda b, h, c: (b * nc + c, h))
    loc_shape = jax.ShapeDtypeStruct((n * t, GDN_W), F32)
    local = pl.pallas_call(
        functools.partial(_gdn_local_kernel, tc, cs, t_valid),
        grid=(n, GDN_HEADS, nc),
        in_specs=[seg_spec(0), seg_spec(1), seg_spec(2),
                  pl.BlockSpec((tc, LANES), lambda b, h, c: (rb0 + b * nc + c, bacol)),
                  cw_spec(0), cw_spec(1), cw_spec(2), cb_spec(0), cb_spec(1), cb_spec(2),
                  pl.BlockSpec((2, LANES), lambda b, h, c: (0, 0))],
        out_specs=[loc_spec] * 5,
        out_shape=[loc_shape] * 5,
        scratch_shapes=[pltpu.VMEM((3, SUBLANES + tc, LANES), F32)],
        compiler_params=_cparams(("parallel", "parallel", "arbitrary")),
        name="gdn_local",
    )(proj, proj, proj, proj, conv_w, conv_w, conv_w, conv_buf, conv_buf, conv_buf, head_params)
    wide = pl.BlockSpec((tc, GDN_W), lambda b, c: (b * nc + c, 0))
    state_spec = pl.BlockSpec((None, GDN_HEADS, GDN_DK, LANES), lambda b, c: (b, 0, 0, 0))
    return pl.pallas_call(
        functools.partial(_gdn_scan_kernel, tc, cs),
        grid=(n, nc),
        in_specs=[wide] * 5 + [
            pl.BlockSpec((tc, GDN_W), lambda b, c: (rb0 + b * nc + c, 3)),
            pl.BlockSpec((1, LANES), lambda b, c: (0, 0)),
            state_spec],
        out_specs=[wide, state_spec],
        out_shape=[loc_shape, jax.ShapeDtypeStruct((n, GDN_HEADS, GDN_DK, LANES), F32)],
        scratch_shapes=[pltpu.VMEM((GDN_HEADS, GDN_DK, LANES), F32)],
        compiler_params=_cparams(("parallel", "arbitrary")),
        name="gdn_scan",
    )(*local, proj, out_norm, s0)


def _rope_tables(pos, rot_dim, head_dim):
    half = rot_dim // 2
    inv_freq = ROPE_THETA ** (-jnp.arange(0, rot_dim, 2, dtype=F32) / rot_dim)
    ang = pos.astype(F32)[:, None] * inv_freq[None, :]
    cos, sin = jnp.cos(ang), jnp.sin(ang)
    r = pos.shape[0]
    rest = head_dim - 2 * half
    zh = jnp.zeros((r, half), F32)
    c = jnp.concatenate([cos, cos, jnp.ones((r, rest), F32)], axis=1)
    sa = jnp.concatenate([-sin, zh, jnp.zeros((r, rest), F32)], axis=1)
    sb = jnp.concatenate([zh, sin, jnp.zeros((r, rest), F32)], axis=1)
    reps = LANES // head_dim
    return tuple(jnp.tile(a, (1, reps)) for a in (c, sa, sb))


def _lane_gain(g):
    return jnp.tile(g.astype(F32), LANES // g.shape[0]).reshape(1, LANES)


def _pick_tile(r, options):
    for tm in options:
        if r % tm == 0:
            return tm
    raise ValueError(f"no row tile for {r}")


def kernel(x_prompt, x_sample, mem_prompt, cache_k, cache_v, cache_idx, cache_mem_k, cache_mem_v,
           state_delta, state_conv, page_table, norm_mix, norm_ffn, attn_q_norm, attn_k_norm,
           mem_q_norm, mem_k_norm, w_in_moba, w_in_dsa, w_in_gdn, gdn_conv, gdn_a_log, gdn_dt_bias,
           gdn_out_norm, w_mem_kv, w_out, w_up, w_down):
    n, t, d = x_prompt.shape
    ns, ts, _ = x_sample.shape
    depth = norm_mix.shape[0]
    npg = page_table.shape[1]
    past = npg * PAGE_SIZE
    mem_len = mem_prompt.shape[1]
    assert d == D_MODEL and t % MOBA_BLOCK == 0 and t >= CONV_W - 1
    assert CONV_W - 1 <= ts <= SAMPLE_ROWS and past % MOBA_BLOCK == 0
    rp = n * t
    rs = ns * SAMPLE_ROWS
    r = -(-(rp + rs) // ROW_TILE) * ROW_TILE
    page_table = page_table.astype(I32)

    xs = jnp.pad(x_sample, ((0, 0), (0, SAMPLE_ROWS - ts), (0, 0))).reshape(rs, d)
    x_all = jnp.concatenate([x_prompt.reshape(rp, d), xs, jnp.zeros((r - rp - rs, d), F32)], axis=0)
    pos_s = past + jnp.minimum(jnp.arange(SAMPLE_ROWS, dtype=I32), ts - 1)
    pos_all = jnp.concatenate([jnp.tile(jnp.arange(t, dtype=I32), n), jnp.tile(pos_s, ns),
                               jnp.zeros((r - rp - rs,), I32)])
    tabs = _rope_tables(pos_all, ROT_DIM, HEAD_DIM)
    itabs = _rope_tables(pos_all, IDX_ROT, IDX_DIM)

    zpad = lambda w, c: jnp.pad(w, ((0, 0), (0, 0), (0, c - w.shape[-1])))
    a3 = 3 * ATT_W
    qi_w = IDX_HEADS * IDX_DIM
    kw_w = IDX_DIM + IDX_HEADS
    w_moba = w_in_moba.astype(BF16)
    w_dsa = zpad(jnp.concatenate([w_in_dsa[..., :a3 + qi_w], w_in_dsa[..., a3 + qi_w + kw_w:],
                                  w_in_dsa[..., a3 + qi_w:a3 + qi_w + kw_w]], axis=-1), DSA_C).astype(BF16)
    g4 = 4 * GDN_W
    w_gdn = zpad(jnp.concatenate([w_in_gdn[..., :g4], w_in_gdn[..., g4 + 2 * GDN_HEADS:],
                                  w_in_gdn[..., g4:g4 + 2 * GDN_HEADS]], axis=-1), GDN_C).astype(BF16)
    w_out_bf = w_out.astype(BF16)
    w_up_bf = w_up.astype(BF16)
    w_down_bf = w_down.astype(BF16)

    kv = layer_matmul(mem_prompt.reshape(n * mem_len, d), w_mem_kv.astype(BF16))
    mk_gain = jnp.tile(mem_k_norm.astype(F32), (1, LANES // HEAD_DIM)).reshape(depth, 1, LANES)
    mem_k_p = mem_key_norm(kv, mk_gain).reshape(depth, n, mem_len, MEM_W)
    mem_v_p = kv[..., MEM_W:].reshape(depth, n, mem_len, MEM_W)
    mem_k_s = cache_mem_k.reshape(depth, ns, mem_len, MEM_W)
    mem_v_s = cache_mem_v.reshape(depth, ns, mem_len, MEM_W)

    tm_ffn = _pick_tile(r, (640, 512, 256))
    tq_mem = _pick_tile(t, (512, 256))
    causal_new = jnp.where((jnp.arange(LANES)[None, :] <= jnp.arange(SAMPLE_ROWS)[:, None])
                           & (jnp.arange(LANES)[None, :] < ts), 0.0, NEG).astype(F32)
    causal_new = jnp.broadcast_to(causal_new, (ns, SAMPLE_ROWS, LANES))
    zero_key_bias = jnp.zeros((ns, SAMPLE_ROWS, past), F32)
    kt_all = jnp.transpose(cache_k, (0, 1, 3, 4, 2)).reshape(cache_k.shape[0], -1, ATT_W, PAGE_SIZE)
    vt_all = jnp.transpose(cache_v, (0, 1, 3, 4, 2)).reshape(cache_v.shape[0], -1, ATT_W, PAGE_SIZE)

    new_k, new_v, new_idx, new_delta_p, new_delta_s, new_conv_p, new_conv_s = [], [], [], [], [], [], []
    kind_count = [0, 0, 0]
    kv_slot = 0
    for l in range(depth):
        kind = l % N_MIXERS
        j = kind_count[kind]
        kind_count[kind] += 1
        w_in = (w_moba, w_dsa, w_gdn)[kind][j]
        proj = norm_matmul(x_all, norm_mix[l], w_in)
        if kind == 2:
            memq_col = 4 * GDN_W // MEM_W
            hp = jnp.zeros((2, LANES), F32)
            hp = hp.at[0, :GDN_HEADS].set(gdn_a_log[j].astype(F32)).at[1, :GDN_HEADS].set(
                gdn_dt_bias[j].astype(F32))
            on = gdn_out_norm[j].astype(F32).reshape(1, LANES)
            kwid = CONV_W - 1
            o_p, sd_p = gdn_mix(proj, gdn_conv[j], jnp.zeros((n, kwid, CONV_CH), F32), hp, on,
                                jnp.zeros((n, GDN_HEADS, GDN_DK, LANES), F32), 0, n, t,
                                MOBA_BLOCK, GDN_CHUNK, None)
            o_s, sd_s = gdn_mix(proj, gdn_conv[j], state_conv[j], hp, on, state_delta[j].astype(F32),
                                rp, ns, SAMPLE_ROWS, SAMPLE_ROWS, SAMPLE_ROWS, ts)
            new_delta_p.append(sd_p)
            new_delta_s.append(sd_s)
            new_conv_p.append(proj[:rp].reshape(n, t, -1)[:, t - kwid:, :CONV_CH])
            new_conv_s.append(proj[rp:rp + rs].reshape(ns, SAMPLE_ROWS, -1)[:, ts - kwid:ts, :CONV_CH])
        else:
            is_dsa = kind == 1
            memq_col = (3 * ATT_W + (MEM_W if is_dsa else 0)) // MEM_W
            outs = qk_post(proj, is_dsa, tabs, itabs, _lane_gain(attn_q_norm[kv_slot]),
                           _lane_gain(attn_k_norm[kv_slot]))
            q_all, k_all, kmean = outs[0], outs[1], outs[2].reshape(r // ROW_TILE, ATT_W)
            if is_dsa:
                qi_all, kit_all = outs[3], outs[4]
                mask = dsa_select_prompt(qi_all, kit_all, proj, n, t)
                o_p = prompt_attention(q_all, k_all, proj, mask, True, n, t)
                key_bias, new_bias = dsa_select_sample(cache_idx[j], page_table, qi_all, kit_all,
                                                       proj, rp, ns, ts)
                new_idx.append(kit_all[:, :IDX_DIM])
            else:
                o_p = prompt_attention(q_all, k_all, proj, kmean, False, n, t)
                key_bias, new_bias = zero_key_bias, causal_new
            o_s = sample_attention(kt_all, vt_all, kv_slot, page_table, q_all, k_all, proj,
                                   key_bias, new_bias, not is_dsa, rp, ns)
            new_k.append(k_all)
            new_v.append(proj[:, 2 * ATT_W:3 * ATT_W])
            kv_slot += 1
        mq_gain = _lane_gain(mem_q_norm[l])
        om_p = mem_attention(proj, memq_col, mq_gain, mem_k_p[l], mem_v_p[l], 0, n, t, tq_mem)
        om_s = mem_attention(proj, memq_col, mq_gain, mem_k_s[l], mem_v_s[l], rp, ns, SAMPLE_ROWS,
                             SAMPLE_ROWS)
        tail = r - rp - rs
        o_all = jnp.concatenate([o_p, o_s, jnp.zeros((tail, ATT_W), F32)], axis=0)
        om_all = jnp.concatenate([om_p, om_s, jnp.zeros((tail, MEM_W), F32)], axis=0)
        x_all = post_block(x_all, o_all, om_all, w_out_bf[l, :ATT_W], w_out_bf[l, ATT_W:], norm_ffn[l],
                           w_up_bf[l], w_down_bf[l], tm_ffn, 1024)

    def split(a, width_shape):
        a = jnp.stack(a)
        ap = a[:, :rp].reshape(a.shape[0], n, t, *width_shape)
        asmp = a[:, rp:rp + rs].reshape(a.shape[0], ns, SAMPLE_ROWS, *width_shape)[:, :, :ts]
        return ap, asmp

    y_prompt = x_all[:rp].reshape(n, t, d)
    y_sample = x_all[rp:rp + rs].reshape(ns, SAMPLE_ROWS, d)[:, :ts]
    k_p, k_s = split(new_k, (N_HEADS, HEAD_DIM))
    v_p, v_s = split(new_v, (N_HEADS, HEAD_DIM))
    idx_p, idx_s = split(new_idx, (IDX_DIM,))
    mem_k_out = mem_k_p.reshape(depth, n, mem_len, MEM_HEADS, HEAD_DIM)
    mem_v_out = mem_v_p.reshape(depth, n, mem_len, MEM_HEADS, HEAD_DIM)
    return (y_prompt, y_sample, k_p, v_p, idx_p, mem_k_out, mem_v_out,
            jnp.stack(new_delta_p), jnp.stack(new_conv_p), k_s, v_s, idx_s,
            jnp.stack(new_delta_s), jnp.stack(new_conv_s))
```

```python
import functools
import math

import jax
import jax.numpy as jnp
from jax import lax
from jax.experimental import pallas as pl
from jax.experimental.pallas import tpu as pltpu

F32 = jnp.float32
BF16 = jnp.bfloat16
I32 = jnp.int32
HI = lax.Precision.HIGHEST

D_MODEL = 1024
HEAD_DIM = 64
N_HEADS = 12
ATT_W = N_HEADS * HEAD_DIM
MEM_HEADS = 4
MEM_W = MEM_HEADS * HEAD_DIM
ROT_DIM = HEAD_DIM // 4
ROPE_THETA = 500000.0
ATT_SCALE = HEAD_DIM ** -0.5
LOG2E = math.log2(math.e)
MOBA_BLOCK = 256
MOBA_TOPK = 3
IDX_HEADS = 8
IDX_DIM = 32
IDX_ROT = IDX_DIM // 4
DSA_TOPK = 256
GDN_HEADS = 6
GDN_DK = 128
GDN_W = GDN_HEADS * GDN_DK
CONV_W = 4
CONV_CH = 3 * GDN_W
GDN_CHUNK = 64
D_FF = 4 * D_MODEL
EPS = 1e-6
NEG = -1e30
PAGE_SIZE = 128
N_MIXERS = 3

LANES = 128
SUBLANES = 8
BF16_ROWS = 16
ROW_TILE = 256
SAMPLE_ROWS = 8
VMEM_LIMIT = 56 * 1024 * 1024
INT_MIN = -2 ** 31
KEYS_PER_ROW = LANES // IDX_DIM
SUB_SPACING_STEPS = (0.5, 0.25, 0.125, 0.0625)
PAGES_PER_STEP = 8

MOBA_C = 3 * ATT_W + MEM_W
DSA_C = 3 * ATT_W + 2 * MEM_W + LANES
GDN_C = 4 * GDN_W + MEM_W + LANES


def _cparams(sem):
    return pltpu.CompilerParams(dimension_semantics=sem, vmem_limit_bytes=VMEM_LIMIT)


def _dot_nt(a, b, precision=None):
    return lax.dot_general(a, b, (((1,), (1,)), ((), ())), precision=precision,
                           preferred_element_type=F32)


def _dot3(a, b, dims):
    a_hi = a.astype(BF16)
    b_hi = b.astype(BF16)
    a_lo = (a - a_hi.astype(F32)).astype(BF16)
    b_lo = (b - b_hi.astype(F32)).astype(BF16)
    dg = lambda x, y: lax.dot_general(x, y, dims, preferred_element_type=F32)
    return dg(a_hi, b_hi) + dg(a_hi, b_lo) + dg(a_lo, b_hi)


def _head_block_ones(width):
    r = lax.broadcasted_iota(I32, (LANES, LANES), 0) // width
    c = lax.broadcasted_iota(I32, (LANES, LANES), 1) // width
    return jnp.where(r == c, 1.0, 0.0).astype(BF16)


def _group_sum(x2, gmat):
    hi = x2.astype(BF16)
    lo = (x2 - hi.astype(F32)).astype(BF16)
    return (jnp.dot(hi, gmat, preferred_element_type=F32)
            + jnp.dot(lo, gmat, preferred_element_type=F32))


def _head_rms(x, gain, gmat, width):
    ms = _group_sum(x * x, gmat) * (1.0 / width)
    return x * lax.rsqrt(ms + EPS) * gain


def _rope(y, c, sa, sb, half):
    return y * c + pltpu.roll(y, LANES - half, 1) * sa + pltpu.roll(y, half, 1) * sb


def _norm_matmul_kernel(x_ref, g_ref, w_ref, o_ref):
    x = x_ref[...]
    ms = jnp.mean(x * x, axis=-1, keepdims=True)
    h = (x * lax.rsqrt(ms + EPS) * g_ref[...]).astype(BF16)
    o_ref[...] = jnp.dot(h, w_ref[...], preferred_element_type=F32)


def norm_matmul(x, g, w_all, j):
    r, d = x.shape
    c = w_all.shape[2]
    tm = ROW_TILE
    return pl.pallas_call(
        _norm_matmul_kernel,
        grid=(r // tm,),
        in_specs=[pl.BlockSpec((tm, d), lambda i: (i, 0)),
                  pl.BlockSpec((1, d), lambda i: (0, 0)),
                  pl.BlockSpec((None, d, c), lambda i: (j, 0, 0))],
        out_specs=pl.BlockSpec((tm, c), lambda i: (i, 0)),
        out_shape=jax.ShapeDtypeStruct((r, c), F32),
        compiler_params=_cparams(("parallel",)),
        name="norm_matmul",
    )(x, g.reshape(1, d), w_all)


def _matmul_kernel(x_ref, w_ref, o_ref):
    o_ref[...] = jnp.dot(x_ref[...].astype(BF16), w_ref[...], preferred_element_type=F32)


def layer_matmul(x, w):
    r, d = x.shape
    nl, _, c = w.shape
    return pl.pallas_call(
        _matmul_kernel,
        grid=(nl,),
        in_specs=[pl.BlockSpec((r, d), lambda l: (0, 0)),
                  pl.BlockSpec((None, d, c), lambda l: (l, 0, 0))],
        out_specs=pl.BlockSpec((None, r, c), lambda l: (l, 0, 0)),
        out_shape=jax.ShapeDtypeStruct((nl, r, c), F32),
        compiler_params=_cparams(("parallel",)),
        name="mem_kv_matmul",
    )(x, w)


def _mem_k_norm_kernel(kv_ref, g_ref, o_ref):
    gmat = _head_block_ones(HEAD_DIM)
    g = g_ref[...]
    for p in range(MEM_W // LANES):
        x = kv_ref[:, p * LANES:(p + 1) * LANES]
        o_ref[:, p * LANES:(p + 1) * LANES] = _head_rms(x, g, gmat, HEAD_DIM)


def mem_key_norm(kv, gains):
    nl, r, _ = kv.shape
    return pl.pallas_call(
        _mem_k_norm_kernel,
        grid=(nl,),
        in_specs=[pl.BlockSpec((None, r, MEM_W), lambda l: (l, 0, 0)),
                  pl.BlockSpec((None, 1, LANES), lambda l: (l, 0, 0))],
        out_specs=pl.BlockSpec((None, r, MEM_W), lambda l: (l, 0, 0)),
        out_shape=jax.ShapeDtypeStruct((nl, r, MEM_W), F32),
        compiler_params=_cparams(("parallel",)),
        name="mem_k_norm",
    )(kv, gains)


def _qk_post_kernel(is_dsa, *refs):
    if is_dsa:
        (q_ref, k_ref, v_ref, qi_ref, kw_ref, c_ref, sa_ref, sb_ref, ci_ref, sai_ref, sbi_ref,
         qg_ref, kg_ref, qo_ref, ko_ref, km_ref, vo_ref, qio_ref, kit_ref) = refs
    else:
        (q_ref, k_ref, v_ref, c_ref, sa_ref, sb_ref, qg_ref, kg_ref,
         qo_ref, ko_ref, km_ref, vo_ref) = refs
    vo_ref[...] = v_ref[...]
    gmat = _head_block_ones(HEAD_DIM)
    c, sa, sb = c_ref[...], sa_ref[...], sb_ref[...]
    qg, kg = qg_ref[...], kg_ref[...]
    half = ROT_DIM // 2
    for p in range(ATT_W // LANES):
        sl = slice(p * LANES, (p + 1) * LANES)
        qn = _rope(_head_rms(q_ref[:, sl], qg, gmat, HEAD_DIM), c, sa, sb, half)
        qo_ref[:, sl] = qn * (ATT_SCALE * LOG2E)
        kn = _rope(_head_rms(k_ref[:, sl], kg, gmat, HEAD_DIM), c, sa, sb, half)
        ko_ref[:, sl] = kn
        km_ref[:, sl] = jnp.sum(kn, axis=0, keepdims=True) * (1.0 / ROW_TILE)
    if is_dsa:
        ci, sai, sbi = ci_ref[...], sai_ref[...], sbi_ref[...]
        ihalf = IDX_ROT // 2
        for p in range(IDX_HEADS * IDX_DIM // LANES):
            sl = slice(p * LANES, (p + 1) * LANES)
            qio_ref[:, sl] = _rope(qi_ref[:, sl], ci, sai, sbi, ihalf)
        kw = kw_ref[...]
        lane = lax.broadcasted_iota(I32, kw.shape, 1)
        kz = jnp.where(lane < IDX_DIM, kw, 0.0)
        kt = kz
        for s in range(1, LANES // IDX_DIM):
            kt = kt + pltpu.roll(kz, s * IDX_DIM, 1)
        kit_ref[...] = _rope(kt, ci, sai, sbi, ihalf)


def qk_post(proj, is_dsa, tabs, itabs, q_gain, k_gain):
    r = proj.shape[0]
    tm = ROW_TILE
    row = lambda cb: (lambda i: (i, cb))
    tab_spec = pl.BlockSpec((tm, LANES), lambda i: (i, 0))
    gain_spec = pl.BlockSpec((1, LANES), lambda i: (0, 0))
    in_specs = [pl.BlockSpec((tm, ATT_W), row(0)), pl.BlockSpec((tm, ATT_W), row(1)),
                pl.BlockSpec((tm, ATT_W), row(2))]
    args = [proj, proj, proj]
    if is_dsa:
        in_specs += [pl.BlockSpec((tm, MEM_W), row(3 * ATT_W // MEM_W)),
                     pl.BlockSpec((tm, LANES), row((3 * ATT_W + 2 * MEM_W) // LANES))]
        args += [proj, proj]
    in_specs += [tab_spec] * 3
    args += list(tabs)
    if is_dsa:
        in_specs += [tab_spec] * 3
        args += list(itabs)
    in_specs += [gain_spec, gain_spec]
    args += [q_gain, k_gain]
    out_specs = [pl.BlockSpec((tm, ATT_W), row(0)), pl.BlockSpec((tm, ATT_W), row(0)),
                 pl.BlockSpec((None, 1, ATT_W), lambda i: (i, 0, 0)),
                 pl.BlockSpec((tm, ATT_W), row(0))]
    out_shape = [jax.ShapeDtypeStruct((r, ATT_W), F32), jax.ShapeDtypeStruct((r, ATT_W), F32),
                 jax.ShapeDtypeStruct((r // tm, 1, ATT_W), F32),
                 jax.ShapeDtypeStruct((r, ATT_W), F32)]
    if is_dsa:
        out_specs += [pl.BlockSpec((tm, MEM_W), row(0)), pl.BlockSpec((tm, LANES), row(0))]
        out_shape += [jax.ShapeDtypeStruct((r, MEM_W), F32), jax.ShapeDtypeStruct((r, LANES), F32)]
    return pl.pallas_call(
        functools.partial(_qk_post_kernel, is_dsa),
        grid=(r // tm,),
        in_specs=in_specs, out_specs=out_specs, out_shape=out_shape,
        compiler_params=_cparams(("parallel",)),
        name="qk_post_dsa" if is_dsa else "qk_post",
    )(*args)


def _post_kernel(x_ref, o_ref, om_ref, wa_ref, wb_ref, g_ref, wu_ref, wd_ref, y_ref, h_sc):
    f = pl.program_id(1)

    @pl.when(f == 0)
    def _():
        x1 = (x_ref[...]
              + jnp.dot(o_ref[...].astype(BF16), wa_ref[...], preferred_element_type=F32)
              + jnp.dot(om_ref[...].astype(BF16), wb_ref[...], preferred_element_type=F32))
        ms = jnp.mean(x1 * x1, axis=-1, keepdims=True)
        h_sc[...] = (x1 * lax.rsqrt(ms + EPS) * g_ref[...]).astype(BF16)
        y_ref[...] = x1

    up = jnp.dot(h_sc[...], wu_ref[...], preferred_element_type=F32)
    act = jnp.square(jnp.maximum(up, 0.0)).astype(BF16)
    y_ref[...] += jnp.dot(act, wd_ref[...], preferred_element_type=F32)


def post_block(x, o, om, w_out, g, w_up, w_down, l, tm, tf):
    r, d = x.shape
    return pl.pallas_call(
        _post_kernel,
        grid=(r // tm, D_FF // tf),
        in_specs=[pl.BlockSpec((tm, d), lambda i, f: (i, 0)),
                  pl.BlockSpec((tm, ATT_W), lambda i, f: (i, 0)),
                  pl.BlockSpec((tm, MEM_W), lambda i, f: (i, 0)),
                  pl.BlockSpec((None, ATT_W, d), lambda i, f: (l, 0, 0)),
                  pl.BlockSpec((None, MEM_W, d), lambda i, f: (l, ATT_W // MEM_W, 0)),
                  pl.BlockSpec((1, d), lambda i, f: (0, 0)),
                  pl.BlockSpec((None, d, tf), lambda i, f: (l, 0, f)),
                  pl.BlockSpec((None, tf, d), lambda i, f: (l, f, 0))],
        out_specs=pl.BlockSpec((tm, d), lambda i, f: (i, 0)),
        out_shape=jax.ShapeDtypeStruct((r, d), F32),
        scratch_shapes=[pltpu.VMEM((tm, d), BF16)],
        compiler_params=_cparams(("parallel", "arbitrary")),
        name="out_proj_ffn",
    )(x, o, om, w_out, w_out, g.reshape(1, d), w_up, w_down)


def _topk_block_bias(gate, n_past, n_top):
    nbk = gate.shape[1]
    col = lax.broadcasted_iota(I32, gate.shape, 1)
    past = col < n_past
    g = jnp.where(past, gate, NEG)
    sel = jnp.zeros(gate.shape, F32)
    for _ in range(n_top):
        mx = jnp.max(g, axis=1, keepdims=True)
        idx = jnp.min(jnp.where(g == mx, col, nbk), axis=1, keepdims=True)
        pick = col == idx
        sel = jnp.where(pick, 1.0, sel)
        g = jnp.where(pick, -jnp.inf, g)
    return jnp.where(past, jnp.where(sel > 0.5, 0.0, NEG), NEG)


def _topk_block_bias_t(gate, n_past, n_top):
    nbk = gate.shape[0]
    row = lax.broadcasted_iota(I32, gate.shape, 0)
    past = row < n_past
    g = jnp.where(past, gate, NEG)
    sel = jnp.zeros(gate.shape, F32)
    for _ in range(n_top):
        mx = jnp.max(g, axis=0, keepdims=True)
        idx = jnp.min(jnp.where(g == mx, row, nbk), axis=0, keepdims=True)
        pick = row == idx
        sel = jnp.where(pick, 1.0, sel)
        g = jnp.where(pick, -jnp.inf, g)
    return jnp.where(past, jnp.where(sel > 0.5, 0.0, NEG), NEG)


def _prompt_attn_kernel(is_dsa, nb, n_top, q_ref, k_ref, v_ref, aux_ref, o_ref,
                        kbf_sc, vt_sc, m_sc, acc_sc, bias_sc):
    blk = MOBA_BLOCK
    i = pl.program_id(2)

    @pl.when(i == 0)
    def _():
        kbf_sc[...] = k_ref[...].astype(BF16)
        ones_row = jnp.where(lax.broadcasted_iota(I32, (BF16_ROWS, blk), 0) == 0, 1.0, 0.0)
        for c in range(nb):
            vt_sc[c, 0:LANES, :] = jnp.transpose(v_ref[c * blk:(c + 1) * blk, :]).astype(BF16)
            vt_sc[c, LANES:LANES + BF16_ROWS, :] = ones_row.astype(BF16)

    q = q_ref[...]
    lane = lax.broadcasted_iota(I32, q.shape, 1)
    qcat = jnp.concatenate([jnp.where(lane // HEAD_DIM == hh, q, 0.0) for hh in range(2)], axis=0)
    qcat_bf = qcat.astype(BF16)
    m_sc[...] = jnp.full(m_sc.shape, NEG, F32)
    acc_sc[...] = jnp.zeros(acc_sc.shape, F32)

    def scores(j):
        st = _dot_nt(kbf_sc[pl.ds(pl.multiple_of(j * blk, blk), blk), :], qcat_bf)
        if is_dsa:
            mb = aux_ref[j].astype(F32)
            st = st + jnp.concatenate([mb, mb], axis=1)
        return st

    def update(js, sts, col_bias):
        m_old = m_sc[...]
        m_new = m_old
        for st, cb in zip(sts, col_bias):
            mx = jnp.max(st, axis=0, keepdims=True)
            m_new = jnp.maximum(m_new, mx if cb is None else mx + cb)
        acc = jnp.exp2(m_old - m_new) * acc_sc[...]
        for j, st, cb in zip(js, sts, col_bias):
            p = jnp.exp2((st - (m_new if cb is None else m_new - cb)).astype(BF16))
            acc = acc + jnp.dot(vt_sc[j], p, preferred_element_type=F32)
        acc_sc[...] = acc
        m_sc[...] = m_new

    if is_dsa:
        col_bias_of = lambda j: None
        count = i + 1
    else:
        gate = _dot_nt(aux_ref[...], qcat, HI)
        bias_sc[...] = _topk_block_bias_t(gate, i, n_top)
        kr = lax.broadcasted_iota(I32, (blk, 2 * blk), 0)
        qc = lax.broadcasted_iota(I32, (blk, 2 * blk), 1) % blk
        update([i], [scores(i) + jnp.where(kr <= qc, 0.0, NEG)], [None])
        col_bias_of = lambda j: bias_sc[pl.ds(j, 1), :]
        count = i

    npairs = count // 2

    def pair(jj, sts):
        nxt = jnp.minimum(jj + 1, npairs - 1)
        nxt_sts = (scores(2 * nxt), scores(2 * nxt + 1))
        js = [2 * jj, 2 * jj + 1]
        update(js, sts, [col_bias_of(j) for j in js])
        return nxt_sts

    lax.fori_loop(0, npairs, pair, (scores(0), scores(1)))

    @pl.when(count % 2 == 1)
    def _():
        update([count - 1], [scores(count - 1)], [col_bias_of(count - 1)])

    acc = acc_sc[...]
    res = acc[0:LANES, :] / acc[LANES:LANES + 1, :]
    drow = lax.broadcasted_iota(I32, (LANES, blk), 0)
    o_t = jnp.where(drow // HEAD_DIM == 0, res[:, :blk], res[:, blk:])
    o_ref[...] = jnp.transpose(o_t)


def prompt_attention(q_all, k_all, proj, aux, is_dsa, n, t):
    blk = MOBA_BLOCK
    nb = t // blk
    npair = ATT_W // LANES
    n_top = min(MOBA_TOPK, nb)
    vcol = 2 * ATT_W // LANES
    if is_dsa:
        aux_spec = pl.BlockSpec((None, None, nb, blk, blk), lambda b, p, i: (b, i, 0, 0, 0))
    else:
        aux_spec = pl.BlockSpec((nb, LANES), lambda b, p, i: (b, p))
    return pl.pallas_call(
        functools.partial(_prompt_attn_kernel, is_dsa, nb, n_top),
        grid=(n, npair, nb),
        in_specs=[pl.BlockSpec((blk, LANES), lambda b, p, i: (b * nb + i, p)),
                  pl.BlockSpec((t, LANES), lambda b, p, i: (b, p)),
                  pl.BlockSpec((t, LANES), lambda b, p, i: (b, vcol + p)),
                  aux_spec],
        out_specs=pl.BlockSpec((blk, LANES), lambda b, p, i: (b * nb + i, p)),
        out_shape=jax.ShapeDtypeStruct((n * t, ATT_W), F32),
        scratch_shapes=[pltpu.VMEM((t, LANES), BF16), pltpu.VMEM((nb, LANES + BF16_ROWS, blk), BF16),
                        pltpu.VMEM((1, 2 * blk), F32),
                        pltpu.VMEM((LANES + BF16_ROWS, 2 * blk), F32), pltpu.VMEM((nb, 2 * blk), F32)],
        compiler_params=_cparams(("parallel", "parallel", "arbitrary")),
        name="prompt_attn_dsa" if is_dsa else "prompt_attn_moba",
    )(q_all, k_all, proj, aux)


def _key_to_float(k):
    b = jnp.where(k < 0, k ^ jnp.int32(0x7FFFFFFF), k)
    return lax.bitcast_convert_type(b, F32)


def _select_topk(count_ge, count_gt, count_tie_lt, k_sel, shape, idx_bits):
    def vbit(b, t):
        cand = t + lax.shift_left(jnp.int32(1), 31 - b)
        return jnp.where(count_ge(_key_to_float(cand)) >= k_sel, cand, t)

    vkey = lax.fori_loop(0, 32, vbit, jnp.full(shape, INT_MIN, I32))
    v = _key_to_float(vkey)
    ulp = _key_to_float(vkey + 1) - v
    for frac in SUB_SPACING_STEPS:
        cand = v + ulp * frac
        v = jnp.where(count_ge(cand) >= k_sel, cand, v)
    need = k_sel - count_gt(v)

    def cbit(b, c):
        cand = c + lax.shift_left(jnp.int32(1), idx_bits - 1 - b)
        return jnp.where(count_tie_lt(v, cand) <= need - 1, cand, c)

    all_ties_taken = jnp.all(count_ge(v) == k_sel)
    c = lax.cond(all_ties_taken,
                 lambda: jnp.full(shape, (1 << idx_bits) - 1, I32),
                 lambda: lax.fori_loop(0, idx_bits, cbit, jnp.zeros(shape, I32)))
    return v, c


def _dsa_select_kernel(nb, k_sel, idx_bits, qi_ref, kit_ref, w_ref, o_ref, sc_sc):
    blk = MOBA_BLOCK
    i = pl.program_id(1)
    w_t = jnp.transpose(w_ref[...])
    lane = lax.broadcasted_iota(I32, (blk, LANES), 1)
    qm, wrow = [], []
    for h in range(IDX_HEADS):
        seg = qi_ref[:, (h // 4) * LANES:(h // 4 + 1) * LANES]
        qm.append(jnp.where(lane // IDX_DIM == h % 4, seg, 0.0).astype(BF16))
        wrow.append(w_t[IDX_DIM + h:IDX_DIM + h + 1, :] * (1.0 / 16.0))
    kr = lax.broadcasted_iota(I32, (blk, blk), 0)
    qpos = i * blk + lax.broadcasted_iota(I32, (blk, blk), 1)

    def score_block(c, carry):
        off = pl.multiple_of(c * blk, blk)
        kt = kit_ref[pl.ds(off, blk), :].astype(BF16)
        acc = jnp.zeros((blk, blk), F32)
        for h in range(IDX_HEADS):
            acc = acc + jnp.maximum(_dot_nt(kt, qm[h]), 0.0) * wrow[h]
        sc_sc[c] = jnp.where(c * blk + kr <= qpos, acc + 0.0, -jnp.inf)
        return carry

    lax.fori_loop(0, i + 1, score_block, 0)

    def fold(x):
        return jnp.sum(x.reshape(blk // SUBLANES, SUBLANES, blk), axis=0)

    def count_ge(cand):
        def body(c, acc):
            return acc + fold(jnp.where(sc_sc[c] >= cand, 1, 0))

        acc = lax.fori_loop(0, i + 1, body, jnp.zeros((SUBLANES, blk), I32))
        return jnp.sum(acc, axis=0, keepdims=True)

    def count_gt(cand):
        def body(c, acc):
            return acc + fold(jnp.where(sc_sc[c] > cand, 1, 0))

        acc = lax.fori_loop(0, i + 1, body, jnp.zeros((SUBLANES, blk), I32))
        return jnp.sum(acc, axis=0, keepdims=True)

    def count_tie_lt(v, cut):
        def body(c, acc):
            hit = (sc_sc[c] == v) & (c * blk + kr < cut)
            return acc + fold(jnp.where(hit, 1, 0))

        acc = lax.fori_loop(0, i + 1, body, jnp.zeros((SUBLANES, blk), I32))
        return jnp.sum(acc, axis=0, keepdims=True)

    v, cut = _select_topk(count_ge, count_gt, count_tie_lt, k_sel, (1, blk), idx_bits)
    take_all = i * blk + lax.broadcasted_iota(I32, (1, blk), 1) + 1 <= k_sel

    def emit(c, carry):
        s = sc_sc[c]
        kpos = c * blk + kr
        bias = jnp.where(s > v, 0.0, jnp.where((s == v) & (kpos <= cut), 0.0, NEG))
        bias = jnp.where(take_all, 0.0, bias)
        o_ref[c] = jnp.where(kpos <= qpos, bias, NEG).astype(BF16)
        return carry

    lax.fori_loop(0, i + 1, emit, 0)

    def fill(c, carry):
        o_ref[c] = jnp.full((blk, blk), NEG, BF16)
        return carry

    lax.fori_loop(i + 1, nb, fill, 0)


def dsa_select_prompt(qi, kit, proj, n, t):
    blk = MOBA_BLOCK
    nb = t // blk
    k_sel = min(DSA_TOPK, t // 4)
    idx_bits = max(1, (t - 1).bit_length())
    wcol = (3 * ATT_W + 2 * MEM_W) // LANES
    return pl.pallas_call(
        functools.partial(_dsa_select_kernel, nb, k_sel, idx_bits),
        grid=(n, nb),
        in_specs=[pl.BlockSpec((blk, MEM_W), lambda b, i: (b * nb + i, 0)),
                  pl.BlockSpec((t, LANES), lambda b, i: (b, 0)),
                  pl.BlockSpec((blk, LANES), lambda b, i: (b * nb + i, wcol))],
        out_specs=pl.BlockSpec((None, None, nb, blk, blk), lambda b, i: (b, i, 0, 0, 0)),
        out_shape=jax.ShapeDtypeStruct((n, nb, nb, blk, blk), BF16),
        scratch_shapes=[pltpu.VMEM((nb, blk, blk), F32)],
        compiler_params=_cparams(("parallel", "arbitrary")),
        name="dsa_select_prompt",
    )(qi, kit, proj)


def _mem_attn_kernel(q_ref, g_ref, mk_ref, mv_ref, o_ref):
    gmat = _head_block_ones(HEAD_DIM)
    g = g_ref[...]
    for p in range(MEM_W // LANES):
        sl = slice(p * LANES, (p + 1) * LANES)
        q = _head_rms(q_ref[:, sl], g, gmat, HEAD_DIM) * ATT_SCALE
        lane = lax.broadcasted_iota(I32, q.shape, 1)
        kb = mk_ref[:, sl].astype(BF16)
        vb = mv_ref[:, sl].astype(BF16)
        outs = []
        for hh in range(2):
            qh = jnp.where(lane // HEAD_DIM == hh, q, 0.0).astype(BF16)
            s = _dot_nt(qh, kb)
            pexp = jnp.exp(s - jnp.max(s, axis=1, keepdims=True))
            den = jnp.sum(pexp, axis=1, keepdims=True)
            outs.append(jnp.dot(pexp.astype(BF16), vb, preferred_element_type=F32) / den)
        o_ref[:, sl] = jnp.where(lane // HEAD_DIM == 0, outs[0], outs[1])


def mem_attention(proj, qcol, gain, mk, mv, row0, n, t, tq):
    nq = t // tq
    rb0 = row0 // tq
    ml = mk.shape[1]
    return pl.pallas_call(
        _mem_attn_kernel,
        grid=(n, nq),
        in_specs=[pl.BlockSpec((tq, MEM_W), lambda b, i: (rb0 + b * nq + i, qcol)),
                  pl.BlockSpec((1, LANES), lambda b, i: (0, 0)),
                  pl.BlockSpec((None, ml, MEM_W), lambda b, i: (b, 0, 0)),
                  pl.BlockSpec((None, ml, MEM_W), lambda b, i: (b, 0, 0))],
        out_specs=pl.BlockSpec((tq, MEM_W), lambda b, i: (b * nq + i, 0)),
        out_shape=jax.ShapeDtypeStruct((n * t, MEM_W), F32),
        compiler_params=_cparams(("parallel", "parallel")),
        name="mem_attn",
    )(proj, gain, mk, mv)


def _tile_rows(x, reps):
    return jnp.concatenate([x] * reps, axis=0)


def _block_diag_queries(q8):
    qt = _tile_rows(q8, N_HEADS)
    row = lax.broadcasted_iota(I32, qt.shape, 0)
    lane = lax.broadcasted_iota(I32, qt.shape, 1)
    return jnp.where(lane // HEAD_DIM == row // SAMPLE_ROWS, qt, 0.0)


def _sample_attn_kernel(nsteps, nbp, n_top, use_gate, *refs):
    npp = PAGES_PER_STEP
    pt_ref = refs[0]
    k_refs, v_refs = refs[1:1 + npp], refs[1 + npp:1 + 2 * npp]
    (q_ref, kn_ref, vn_ref, km_ref, nbias_ref, o_ref,
     qbd_sc, kmean_sc, m_sc, l_sc, acc_sc) = refs[1 + 2 * npp:]
    j = pl.program_id(1)
    ppb = MOBA_BLOCK // PAGE_SIZE

    @pl.when(j == 0)
    def _():
        qbd_sc[...] = _block_diag_queries(q_ref[...])
        kmean_sc[...] = jnp.zeros(kmean_sc.shape, F32)

    qbd = qbd_sc[...]
    qbd_bf = qbd.astype(BF16)
    kmask = km_ref[...]
    ones = jnp.ones((PAGE_SIZE, LANES), BF16)
    lane = lax.broadcasted_iota(I32, (ATT_W, LANES), 1)
    for blk in range(npp // ppb):
        bi = (npp // ppb) * j + blk
        scores, ksum = [], jnp.zeros((ATT_W, LANES), F32)
        for pg in range(ppb):
            kf = k_refs[blk * ppb + pg][...]
            khi = kf.astype(BF16)
            scores.append(jnp.dot(qbd_bf, khi, preferred_element_type=F32))
            if use_gate:
                klo = (kf - khi.astype(F32)).astype(BF16)
                ksum = (ksum + jnp.dot(khi, ones, preferred_element_type=F32)
                        + jnp.dot(klo, ones, preferred_element_type=F32))
        if use_gate:
            kmean_sc[...] += jnp.where(lane == bi, ksum * (1.0 / MOBA_BLOCK), 0.0)
        s = jnp.concatenate(scores, axis=1)
        s = s + _tile_rows(kmask[:, blk * MOBA_BLOCK:(blk + 1) * MOBA_BLOCK], N_HEADS)
        m = jnp.max(s, axis=1, keepdims=True)
        p = jnp.exp2(s - m)
        pb = p.astype(BF16)
        acc = jnp.zeros((N_HEADS * SAMPLE_ROWS, ATT_W), F32)
        for pg in range(ppb):
            vb = v_refs[blk * ppb + pg][...].astype(BF16)
            acc = acc + _dot_nt(pb[:, pg * PAGE_SIZE:(pg + 1) * PAGE_SIZE], vb)
        m_sc[bi] = m
        l_sc[bi] = jnp.sum(p, axis=1, keepdims=True)
        acc_sc[bi] = acc

    @pl.when(j == nsteps - 1)
    def _():
        pad = jnp.zeros((PAGE_SIZE - SAMPLE_ROWS, ATT_W), F32)
        kn = jnp.concatenate([kn_ref[...], pad], axis=0).astype(BF16)
        vn = jnp.concatenate([vn_ref[...], pad], axis=0).astype(BF16)
        s = _dot_nt(qbd_bf, kn) + _tile_rows(nbias_ref[...], N_HEADS)
        m_n = jnp.max(s, axis=1, keepdims=True)
        p = jnp.exp2(s - m_n)
        l_n = jnp.sum(p, axis=1, keepdims=True)
        acc_n = jnp.dot(p.astype(BF16), vn, preferred_element_type=F32)
        if use_gate:
            gate = jnp.dot(qbd, kmean_sc[...], precision=HI, preferred_element_type=F32)
            bias = _topk_block_bias(gate, nbp, n_top)
            mb = [m_sc[b] + bias[:, b:b + 1] for b in range(nbp)]
        else:
            mb = [m_sc[b] for b in range(nbp)]
        m_all = m_n
        for b in range(nbp):
            m_all = jnp.maximum(m_all, mb[b])
        w_n = jnp.exp2(m_n - m_all)
        den = w_n * l_n
        num = w_n * acc_n
        for b in range(nbp):
            w_b = jnp.exp2(mb[b] - m_all)
            den = den + w_b * l_sc[b]
            num = num + w_b * acc_sc[b]
        res = num / den
        olane = lax.broadcasted_iota(I32, (SAMPLE_ROWS, ATT_W), 1)
        out = jnp.zeros((SAMPLE_ROWS, ATT_W), F32)
        for h in range(N_HEADS):
            out = out + jnp.where(olane // HEAD_DIM == h,
                                  res[h * SAMPLE_ROWS:(h + 1) * SAMPLE_ROWS, :], 0.0)
        o_ref[...] = out


def sample_attention(kt_all, vt_all, slot, page_table, q_all, k_all, proj, key_bias, new_bias,
                     use_gate, row0, ns):
    npg = page_table.shape[1]
    npp = PAGES_PER_STEP
    assert npg % npp == 0
    nsteps = npg // npp
    nbp = npg // (MOBA_BLOCK // PAGE_SIZE)
    assert nbp <= LANES
    n_top = min(MOBA_TOPK, nbp + 1)
    rb0 = row0 // SAMPLE_ROWS
    rows = N_HEADS * SAMPLE_ROWS
    page = lambda off: pl.BlockSpec((None, None, ATT_W, PAGE_SIZE),
                                    lambda b, j, pt: (slot, pt[b, npp * j + off], 0, 0))
    rowblk = lambda cb: pl.BlockSpec((SAMPLE_ROWS, ATT_W), lambda b, j, pt: (rb0 + b, cb))
    grid_spec = pltpu.PrefetchScalarGridSpec(
        num_scalar_prefetch=1,
        grid=(ns, nsteps),
        in_specs=[page(off) for off in range(npp)] * 2 + [
            rowblk(0), rowblk(0), rowblk(2),
            pl.BlockSpec((None, SAMPLE_ROWS, npp * PAGE_SIZE), lambda b, j, pt: (b, 0, j)),
            pl.BlockSpec((None, SAMPLE_ROWS, LANES), lambda b, j, pt: (b, 0, 0))],
        out_specs=pl.BlockSpec((SAMPLE_ROWS, ATT_W), lambda b, j, pt: (b, 0)),
        scratch_shapes=[pltpu.VMEM((rows, ATT_W), F32), pltpu.VMEM((ATT_W, LANES), F32),
                        pltpu.VMEM((nbp, rows, 1), F32), pltpu.VMEM((nbp, rows, 1), F32),
                        pltpu.VMEM((nbp, rows, ATT_W), F32)],
    )
    return pl.pallas_call(
        functools.partial(_sample_attn_kernel, nsteps, nbp, n_top, use_gate),
        grid_spec=grid_spec,
        out_shape=jax.ShapeDtypeStruct((ns * SAMPLE_ROWS, ATT_W), F32),
        compiler_params=_cparams(("parallel", "arbitrary")),
        name="sample_attn_moba" if use_gate else "sample_attn_dsa",
    )(page_table, *([kt_all] * npp), *([vt_all] * npp), q_all, k_all, proj, key_bias, new_bias)


def _dsa_select_sample_kernel(npg, ts, k_sel, idx_bits, pt_ref, cidx_ref, qi_ref, kit_ref, w_ref,
                              km_ref, nb_ref, kbuf, sem):
    b = pl.program_id(0)
    past = npg * PAGE_SIZE
    kpr = PAGE_SIZE // KEYS_PER_ROW
    nrow = past // KEYS_PER_ROW

    def page_copy(p):
        return pltpu.make_async_copy(cidx_ref.at[pt_ref[b, p]], kbuf.at[pl.ds(p * kpr, kpr)], sem)

    for p in range(npg):
        page_copy(p).start()
    for p in range(npg):
        page_copy(p).wait()

    lane = lax.broadcasted_iota(I32, (SAMPLE_ROWS, LANES), 1)
    wblk = w_ref[...]
    kpast = kbuf[...].astype(BF16)
    knew = jnp.where(lane < IDX_DIM, kit_ref[...], 0.0)
    knew = jnp.concatenate([knew, jnp.zeros((LANES - SAMPLE_ROWS, LANES), F32)], axis=0).astype(BF16)
    isc_p = [jnp.zeros((SAMPLE_ROWS, nrow), F32) for _ in range(KEYS_PER_ROW)]
    isc_n = jnp.zeros((SAMPLE_ROWS, LANES), F32)
    for h in range(IDX_HEADS):
        seg = qi_ref[:, (h // 4) * LANES:(h // 4 + 1) * LANES]
        w_h = jnp.sum(jnp.where(lane == IDX_DIM + h, wblk, 0.0), axis=1, keepdims=True) * (1.0 / 16.0)
        for c in range(KEYS_PER_ROW):
            shift = ((c - h % 4) * IDX_DIM) % LANES
            qs = pltpu.roll(seg, shift, 1) if shift else seg
            qmh = jnp.where(lane // IDX_DIM == c, qs, 0.0).astype(BF16)
            isc_p[c] = isc_p[c] + jnp.maximum(_dot_nt(qmh, kpast), 0.0) * w_h
            if c == 0:
                isc_n = isc_n + jnp.maximum(_dot_nt(qmh, knew), 0.0) * w_h
    key_p = [x + 0.0 for x in isc_p]
    trow = lax.broadcasted_iota(I32, (SAMPLE_ROWS, LANES), 0)
    valid_n = (lane <= trow) & (lane < ts)
    key_n = jnp.where(valid_n, isc_n + 0.0, -jnp.inf)
    rid = lax.broadcasted_iota(I32, (SAMPLE_ROWS, nrow), 1)
    idx_p = [KEYS_PER_ROW * rid + c for c in range(KEYS_PER_ROW)]
    idx_n = past + lane
    cnt = lambda m: jnp.sum(jnp.where(m, 1, 0), axis=1, keepdims=True)

    def count_ge(cand):
        return sum([cnt(k >= cand) for k in key_p], cnt(key_n >= cand))

    def count_gt(cand):
        return sum([cnt(k > cand) for k in key_p], cnt(key_n > cand))

    def count_tie_lt(v, cut):
        return sum([cnt((k == v) & (ix < cut)) for k, ix in zip(key_p, idx_p)],
                   cnt((key_n == v) & (idx_n < cut)))

    v, cut = _select_topk(count_ge, count_gt, count_tie_lt, k_sel, (SAMPLE_ROWS, 1), idx_bits)
    trow1 = lax.broadcasted_iota(I32, (SAMPLE_ROWS, 1), 0)
    take_all = past + jnp.minimum(trow1, ts - 1) + 1 <= k_sel
    for c in range(KEYS_PER_ROW):
        sel = (key_p[c] > v) | ((key_p[c] == v) & (idx_p[c] <= cut))
        km_ref[c] = jnp.where(take_all, 0.0, jnp.where(sel, 0.0, NEG))
    sel_n = (key_n > v) | ((key_n == v) & (idx_n <= cut))
    nb_ref[...] = jnp.where(valid_n, jnp.where(take_all, 0.0, jnp.where(sel_n, 0.0, NEG)), NEG)


def dsa_select_sample(cache_idx_slot, page_table, qi, kit, proj, row0, ns, ts):
    npg = page_table.shape[1]
    past = npg * PAGE_SIZE
    k_sel = min(DSA_TOPK, (past + ts) // 4)
    idx_bits = max(1, (past + LANES - 1).bit_length())
    rb0 = row0 // SAMPLE_ROWS
    wcol = (3 * ATT_W + 2 * MEM_W) // LANES
    nrow = past // KEYS_PER_ROW
    pages = cache_idx_slot.reshape(-1, PAGE_SIZE // KEYS_PER_ROW, LANES)
    grid_spec = pltpu.PrefetchScalarGridSpec(
        num_scalar_prefetch=1,
        grid=(ns,),
        in_specs=[pl.BlockSpec(memory_space=pl.ANY),
                  pl.BlockSpec((SAMPLE_ROWS, MEM_W), lambda b, pt: (rb0 + b, 0)),
                  pl.BlockSpec((SAMPLE_ROWS, LANES), lambda b, pt: (rb0 + b, 0)),
                  pl.BlockSpec((SAMPLE_ROWS, LANES), lambda b, pt: (rb0 + b, wcol))],
        out_specs=[pl.BlockSpec((None, KEYS_PER_ROW, SAMPLE_ROWS, nrow), lambda b, pt: (b, 0, 0, 0)),
                   pl.BlockSpec((None, SAMPLE_ROWS, LANES), lambda b, pt: (b, 0, 0))],
        scratch_shapes=[pltpu.VMEM((nrow, LANES), F32), pltpu.SemaphoreType.DMA(())],
    )
    km, nbias = pl.pallas_call(
        functools.partial(_dsa_select_sample_kernel, npg, ts, k_sel, idx_bits),
        grid_spec=grid_spec,
        out_shape=[jax.ShapeDtypeStruct((ns, KEYS_PER_ROW, SAMPLE_ROWS, nrow), F32),
                   jax.ShapeDtypeStruct((ns, SAMPLE_ROWS, LANES), F32)],
        compiler_params=_cparams(("arbitrary",)),
        name="dsa_select_sample",
    )(page_table, pages, qi, kit, proj)
    return jnp.transpose(km, (0, 2, 3, 1)).reshape(ns, SAMPLE_ROWS, past), nbias


def _gdn_local_kernel(tc, cs, hps, t_valid, q_ref, k_ref, v_ref, ba_ref, cwq_ref, cwk_ref, cwv_ref,
                      cbq_ref, cbk_ref, cbv_ref, hp_ref, u_ref, w_ref, qd_ref, kd_ref, at_ref, xp_sc):
    h0 = pl.program_id(1) * hps
    c = pl.program_id(2)
    kw = CONV_W - 1

    @pl.when(c == 0)
    def _():
        for seg, cb in enumerate((cbq_ref, cbk_ref, cbv_ref)):
            xp_sc[seg, SUBLANES - kw:SUBLANES, :] = cb[...]

    acts = []
    for seg, (xr, cw) in enumerate(((q_ref, cwq_ref), (k_ref, cwk_ref), (v_ref, cwv_ref))):
        xp_sc[seg, SUBLANES:SUBLANES + tc, :] = xr[...]
        y = xp_sc[seg, SUBLANES - kw:SUBLANES - kw + tc, :] * cw[0:1, :]
        for jj in range(1, CONV_W):
            y = y + xp_sc[seg, SUBLANES - kw + jj:SUBLANES - kw + jj + tc, :] * cw[jj:jj + 1, :]
        acts.append(jax.nn.silu(y))
        xp_sc[seg, SUBLANES - kw:SUBLANES, :] = xp_sc[seg, SUBLANES + tc - kw:SUBLANES + tc, :]
    lane = lax.broadcasted_iota(I32, (tc, LANES), 1)
    lane1 = lax.broadcasted_iota(I32, (1, LANES), 1)
    ba = ba_ref[...]
    hp = hp_ref[...]
    qn, kn, va, beta, g = [], [], [], [], []
    for hh in range(hps):
        cl = slice(hh * LANES, (hh + 1) * LANES)
        h = h0 + hh
        qa, ka = acts[0][:, cl], acts[1][:, cl]
        qn.append(qa * lax.rsqrt(jnp.sum(qa * qa, axis=-1, keepdims=True) + EPS) * (GDN_DK ** -0.5))
        kn.append(ka * lax.rsqrt(jnp.sum(ka * ka, axis=-1, keepdims=True) + EPS))
        va.append(acts[2][:, cl])
        bcol = jnp.sum(jnp.where(lane == h, ba, 0.0), axis=1, keepdims=True)
        acol = jnp.sum(jnp.where(lane == GDN_HEADS + h, ba, 0.0), axis=1, keepdims=True)
        alog = jnp.sum(jnp.where(lane1 == h, hp[0:1, :], 0.0), axis=1, keepdims=True)
        dtb = jnp.sum(jnp.where(lane1 == h, hp[1:2, :], 0.0), axis=1, keepdims=True)
        beta_h = jax.nn.sigmoid(bcol)
        g_h = -jnp.exp(alog) * jax.nn.softplus(acol + dtb)
        if t_valid is not None:
            rowid = c * tc + lax.broadcasted_iota(I32, (tc, 1), 0)
            beta_h = jnp.where(rowid < t_valid, beta_h, 0.0)
            g_h = jnp.where(rowid < t_valid, g_h, 0.0)
        beta.append(beta_h)
        g.append(g_h)

    ri = lax.broadcasted_iota(I32, (cs, cs), 0)
    ci = lax.broadcasted_iota(I32, (cs, cs), 1)
    tri = ri >= ci
    stri = ri > ci
    tril = jnp.where(tri, 1.0, 0.0)
    eye = jnp.where(ri == ci, 1.0, 0.0)
    e0 = jnp.where(lax.broadcasted_iota(I32, (cs, LANES), 1) == 0, 1.0, 0.0)
    mm = lambda a, b: _dot3(a, b, (((1,), (0,)), ((), ())))
    mm_nt = lambda a, b: _dot3(a, b, (((1,), (1,)), ((), ())))
    n_sq = max(0, (cs - 1).bit_length() - 1)

    units = [(hh, slice(ch * cs, (ch + 1) * cs)) for hh in range(hps) for ch in range(tc // cs)]
    gcum = [jnp.dot(tril, jnp.broadcast_to(g[hh][rs], (cs, LANES)), precision=HI,
                    preferred_element_type=F32) for hh, rs in units]
    grow = [_dot_nt(e0, x, HI) for x in gcum]
    decay = [jnp.where(tri, jnp.exp(jnp.where(tri, gc[:, 0:cs] - gr, 0.0)), 0.0)
             for gc, gr in zip(gcum, grow)]
    kb = [kn[hh][rs] * beta[hh][rs] for hh, rs in units]
    pw = [-jnp.where(stri, mm_nt(kbc, kn[hh][rs]) * dc, 0.0)
          for kbc, (hh, rs), dc in zip(kb, units, decay)]
    tinv = [eye + p for p in pw]
    for _ in range(n_sq):
        pw = [mm(p, p) for p in pw]
        tinv = [t + mm(t, p) for t, p in zip(tinv, pw)]
    egc = [jnp.exp(x) for x in gcum]
    for i, (hh, rs) in enumerate(units):
        cl = slice(hh * LANES, (hh + 1) * LANES)
        glast = gcum[i][cs - 1:cs, :]
        u_ref[rs, cl] = mm(tinv[i], va[hh][rs] * beta[hh][rs])
        w_ref[rs, cl] = mm(tinv[i], kb[i] * egc[i])
        qd_ref[rs, cl] = qn[hh][rs] * egc[i]
        kd_ref[rs, cl] = kn[hh][rs] * jnp.exp(glast - gcum[i])
        at = jnp.where(tri, mm_nt(qn[hh][rs], kn[hh][rs]) * decay[i], 0.0)
        at_ref[rs, cl] = jnp.zeros((cs, LANES), F32)
        at_ref[rs, hh * LANES:hh * LANES + cs] = at
        at_ref[rs, hh * LANES + LANES // 2:(hh + 1) * LANES] = jnp.broadcast_to(
            jnp.exp(glast), (cs, LANES))[:, LANES // 2:]


def _gdn_scan_kernel(tc, cs, u_ref, w_ref, qd_ref, kd_ref, at_ref, z_ref, on_ref, s0_ref,
                     o_ref, so_ref, s_sc):
    c = pl.program_id(1)
    nc = pl.num_programs(1)

    @pl.when(c == 0)
    def _():
        s_sc[...] = s0_ref[...]

    mm = lambda a, b: _dot3(a, b, (((1,), (0,)), ((), ())))
    lane = lax.broadcasted_iota(I32, (cs, LANES), 1)
    on = on_ref[...]
    heads = [slice(h * LANES, (h + 1) * LANES) for h in range(GDN_HEADS)]
    state = [s_sc[h] for h in range(GDN_HEADS)]
    for ch in range(tc // cs):
        rs = slice(ch * cs, (ch + 1) * cs)
        at = [at_ref[rs, cl] for cl in heads]
        v_new = [u_ref[rs, cl] - mm(w_ref[rs, cl], s) for cl, s in zip(heads, state)]
        o = [mm(qd_ref[rs, cl], s) + mm(a[:, 0:cs], vn)
             for cl, s, a, vn in zip(heads, state, at, v_new)]
        decay_tot = [jnp.where(lane < LANES // 2, pltpu.roll(a, LANES // 2, 1), a)[0:1, :] for a in at]
        state = [s * dt + _dot3(kd_ref[rs, cl], vn, (((0,), (0,)), ((), ())))
                 for cl, s, dt, vn in zip(heads, state, decay_tot, v_new)]
        for cl, oh in zip(heads, o):
            ms = jnp.mean(oh * oh, axis=-1, keepdims=True)
            o_ref[rs, cl] = oh * lax.rsqrt(ms + EPS) * on * jax.nn.silu(z_ref[rs, cl])
    for h in range(GDN_HEADS):
        s_sc[h] = state[h]

    @pl.when(c == nc - 1)
    def _():
        so_ref[...] = s_sc[...]


def gdn_mix(proj, conv_w, conv_buf, head_params, out_norm, s0, row0, n, t, tc, cs, hps, t_valid):
    nc = t // tc
    rb0 = row0 // tc
    bacol = (4 * GDN_W + MEM_W) // LANES
    kw = CONV_W - 1
    assert cs <= LANES // 2 and GDN_HEADS % hps == 0
    hw = hps * LANES
    hb = GDN_W // hw
    seg_spec = lambda s: pl.BlockSpec((tc, hw), lambda b, h, c: (rb0 + b * nc + c, s * hb + h))
    cw_spec = lambda s: pl.BlockSpec((CONV_W, hw), lambda b, h, c: (0, s * hb + h))
    cb_spec = lambda s: pl.BlockSpec((None, kw, hw), lambda b, h, c: (b, 0, s * hb + h))
    loc_spec = pl.BlockSpec((tc, hw), lambda b, h, c: (b * nc + c, h))
    loc_shape = jax.ShapeDtypeStruct((n * t, GDN_W), F32)
    local = pl.pallas_call(
        functools.partial(_gdn_local_kernel, tc, cs, hps, t_valid),
        grid=(n, hb, nc),
        in_specs=[seg_spec(0), seg_spec(1), seg_spec(2),
                  pl.BlockSpec((tc, LANES), lambda b, h, c: (rb0 + b * nc + c, bacol)),
                  cw_spec(0), cw_spec(1), cw_spec(2), cb_spec(0), cb_spec(1), cb_spec(2),
                  pl.BlockSpec((2, LANES), lambda b, h, c: (0, 0))],
        out_specs=[loc_spec] * 5,
        out_shape=[loc_shape] * 5,
        scratch_shapes=[pltpu.VMEM((3, SUBLANES + tc, hw), F32)],
        compiler_params=_cparams(("parallel", "parallel", "arbitrary")),
        name="gdn_local",
    )(proj, proj, proj, proj, conv_w, conv_w, conv_w, conv_buf, conv_buf, conv_buf, head_params)
    wide = pl.BlockSpec((tc, GDN_W), lambda b, c: (b * nc + c, 0))
    state_spec = pl.BlockSpec((None, GDN_HEADS, GDN_DK, LANES), lambda b, c: (b, 0, 0, 0))
    return pl.pallas_call(
        functools.partial(_gdn_scan_kernel, tc, cs),
        grid=(n, nc),
        in_specs=[wide] * 5 + [
            pl.BlockSpec((tc, GDN_W), lambda b, c: (rb0 + b * nc + c, 3)),
            pl.BlockSpec((1, LANES), lambda b, c: (0, 0)),
            state_spec],
        out_specs=[wide, state_spec],
        out_shape=[loc_shape, jax.ShapeDtypeStruct((n, GDN_HEADS, GDN_DK, LANES), F32)],
        scratch_shapes=[pltpu.VMEM((GDN_HEADS, GDN_DK, LANES), F32)],
        compiler_params=_cparams(("parallel", "arbitrary")),
        name="gdn_scan",
    )(*local, proj, out_norm, s0)


def _rope_tables(pos, rot_dim, head_dim):
    half = rot_dim // 2
    inv_freq = ROPE_THETA ** (-jnp.arange(0, rot_dim, 2, dtype=F32) / rot_dim)
    ang = pos.astype(F32)[:, None] * inv_freq[None, :]
    cos, sin = jnp.cos(ang), jnp.sin(ang)
    r = pos.shape[0]
    rest = head_dim - 2 * half
    zh = jnp.zeros((r, half), F32)
    c = jnp.concatenate([cos, cos, jnp.ones((r, rest), F32)], axis=1)
    sa = jnp.concatenate([-sin, zh, jnp.zeros((r, rest), F32)], axis=1)
    sb = jnp.concatenate([zh, sin, jnp.zeros((r, rest), F32)], axis=1)
    reps = LANES // head_dim
    return tuple(jnp.tile(a, (1, reps)) for a in (c, sa, sb))


def _lane_gain(g):
    return jnp.tile(g.astype(F32), LANES // g.shape[0]).reshape(1, LANES)


def _pick_tile(r, options):
    for tm in options:
        if r % tm == 0:
            return tm
    raise ValueError(f"no row tile for {r}")


def kernel(x_prompt, x_sample, mem_prompt, cache_k, cache_v, cache_idx, cache_mem_k, cache_mem_v,
           state_delta, state_conv, page_table, norm_mix, norm_ffn, attn_q_norm, attn_k_norm,
           mem_q_norm, mem_k_norm, w_in_moba, w_in_dsa, w_in_gdn, gdn_conv, gdn_a_log, gdn_dt_bias,
           gdn_out_norm, w_mem_kv, w_out, w_up, w_down):
    n, t, d = x_prompt.shape
    ns, ts, _ = x_sample.shape
    depth = norm_mix.shape[0]
    npg = page_table.shape[1]
    past = npg * PAGE_SIZE
    mem_len = mem_prompt.shape[1]
    assert d == D_MODEL and t % MOBA_BLOCK == 0 and t >= CONV_W - 1
    assert CONV_W - 1 <= ts <= SAMPLE_ROWS and past % MOBA_BLOCK == 0
    rp = n * t
    rs = ns * SAMPLE_ROWS
    r = -(-(rp + rs) // ROW_TILE) * ROW_TILE
    page_table = page_table.astype(I32)

    xs = jnp.pad(x_sample, ((0, 0), (0, SAMPLE_ROWS - ts), (0, 0))).reshape(rs, d)
    x_all = jnp.concatenate([x_prompt.reshape(rp, d), xs, jnp.zeros((r - rp - rs, d), F32)], axis=0)
    pos_s = past + jnp.minimum(jnp.arange(SAMPLE_ROWS, dtype=I32), ts - 1)
    pos_all = jnp.concatenate([jnp.tile(jnp.arange(t, dtype=I32), n), jnp.tile(pos_s, ns),
                               jnp.zeros((r - rp - rs,), I32)])
    tabs = _rope_tables(pos_all, ROT_DIM, HEAD_DIM)
    itabs = _rope_tables(pos_all, IDX_ROT, IDX_DIM)

    zpad = lambda w, c: jnp.pad(w, ((0, 0), (0, 0), (0, c - w.shape[-1])))
    a3 = 3 * ATT_W
    qi_w = IDX_HEADS * IDX_DIM
    kw_w = IDX_DIM + IDX_HEADS
    w_moba = w_in_moba.astype(BF16)
    w_dsa = zpad(jnp.concatenate([w_in_dsa[..., :a3 + qi_w], w_in_dsa[..., a3 + qi_w + kw_w:],
                                  w_in_dsa[..., a3 + qi_w:a3 + qi_w + kw_w]], axis=-1), DSA_C).astype(BF16)
    g4 = 4 * GDN_W
    w_gdn = zpad(jnp.concatenate([w_in_gdn[..., :g4], w_in_gdn[..., g4 + 2 * GDN_HEADS:],
                                  w_in_gdn[..., g4:g4 + 2 * GDN_HEADS]], axis=-1), GDN_C).astype(BF16)
    w_out_bf = w_out.astype(BF16)
    w_up_bf = w_up.astype(BF16)
    w_down_bf = w_down.astype(BF16)

    kv = layer_matmul(mem_prompt.reshape(n * mem_len, d), w_mem_kv.astype(BF16))
    mk_gain = jnp.tile(mem_k_norm.astype(F32), (1, LANES // HEAD_DIM)).reshape(depth, 1, LANES)
    mem_k_p = mem_key_norm(kv, mk_gain).reshape(depth, n, mem_len, MEM_W)
    mem_v_p = kv[..., MEM_W:].reshape(depth, n, mem_len, MEM_W)
    mem_k_s = cache_mem_k.reshape(depth, ns, mem_len, MEM_W)
    mem_v_s = cache_mem_v.reshape(depth, ns, mem_len, MEM_W)

    tm_ffn = _pick_tile(r, (640, 512, 256))
    tq_mem = _pick_tile(t, (512, 256))
    causal_new = jnp.where((jnp.arange(LANES)[None, :] <= jnp.arange(SAMPLE_ROWS)[:, None])
                           & (jnp.arange(LANES)[None, :] < ts), 0.0, NEG).astype(F32)
    causal_new = jnp.broadcast_to(causal_new, (ns, SAMPLE_ROWS, LANES))
    zero_key_bias = jnp.zeros((ns, SAMPLE_ROWS, past), F32)
    kt_all = jnp.transpose(cache_k, (0, 1, 3, 4, 2)).reshape(cache_k.shape[0], -1, ATT_W, PAGE_SIZE)
    vt_all = jnp.transpose(cache_v, (0, 1, 3, 4, 2)).reshape(cache_v.shape[0], -1, ATT_W, PAGE_SIZE)

    new_k, new_v, new_idx, new_delta_p, new_delta_s, new_conv_p, new_conv_s = [], [], [], [], [], [], []
    kind_count = [0, 0, 0]
    kv_slot = 0
    for l in range(depth):
        kind = l % N_MIXERS
        j = kind_count[kind]
        kind_count[kind] += 1
        proj = norm_matmul(x_all, norm_mix[l], (w_moba, w_dsa, w_gdn)[kind], j)
        if kind == 2:
            memq_col = 4 * GDN_W // MEM_W
            hp = jnp.zeros((2, LANES), F32)
            hp = hp.at[0, :GDN_HEADS].set(gdn_a_log[j].astype(F32)).at[1, :GDN_HEADS].set(
                gdn_dt_bias[j].astype(F32))
            on = gdn_out_norm[j].astype(F32).reshape(1, LANES)
            kwid = CONV_W - 1
            o_p, sd_p = gdn_mix(proj, gdn_conv[j], jnp.zeros((n, kwid, CONV_CH), F32), hp, on,
                                jnp.zeros((n, GDN_HEADS, GDN_DK, LANES), F32), 0, n, t,
                                MOBA_BLOCK, GDN_CHUNK, 3, None)
            o_s, sd_s = gdn_mix(proj, gdn_conv[j], state_conv[j], hp, on, state_delta[j].astype(F32),
                                rp, ns, SAMPLE_ROWS, SAMPLE_ROWS, SAMPLE_ROWS, GDN_HEADS, ts)
            new_delta_p.append(sd_p)
            new_delta_s.append(sd_s)
            new_conv_p.append(proj[:rp].reshape(n, t, -1)[:, t - kwid:, :CONV_CH])
            new_conv_s.append(proj[rp:rp + rs].reshape(ns, SAMPLE_ROWS, -1)[:, ts - kwid:ts, :CONV_CH])
        else:
            is_dsa = kind == 1
            memq_col = (3 * ATT_W + (MEM_W if is_dsa else 0)) // MEM_W
            outs = qk_post(proj, is_dsa, tabs, itabs, _lane_gain(attn_q_norm[kv_slot]),
                           _lane_gain(attn_k_norm[kv_slot]))
            q_all, k_all, kmean = outs[0], outs[1], outs[2].reshape(r // ROW_TILE, ATT_W)
            if is_dsa:
                qi_all, kit_all = outs[4], outs[5]
                mask = dsa_select_prompt(qi_all, kit_all, proj, n, t)
                o_p = prompt_attention(q_all, k_all, proj, mask, True, n, t)
                key_bias, new_bias = dsa_select_sample(cache_idx[j], page_table, qi_all, kit_all,
                                                       proj, rp, ns, ts)
                new_idx.append(kit_all[:, :IDX_DIM])
            else:
                o_p = prompt_attention(q_all, k_all, proj, kmean, False, n, t)
                key_bias, new_bias = zero_key_bias, causal_new
            o_s = sample_attention(kt_all, vt_all, kv_slot, page_table, q_all, k_all, proj,
                                   key_bias, new_bias, not is_dsa, rp, ns)
            new_k.append(k_all)
            new_v.append(outs[3])
            kv_slot += 1
        mq_gain = _lane_gain(mem_q_norm[l])
        om_p = mem_attention(proj, memq_col, mq_gain, mem_k_p[l], mem_v_p[l], 0, n, t, tq_mem)
        om_s = mem_attention(proj, memq_col, mq_gain, mem_k_s[l], mem_v_s[l], rp, ns, SAMPLE_ROWS,
                             SAMPLE_ROWS)
        tail = r - rp - rs
        o_all = jnp.concatenate([o_p, o_s, jnp.zeros((tail, ATT_W), F32)], axis=0)
        om_all = jnp.concatenate([om_p, om_s, jnp.zeros((tail, MEM_W), F32)], axis=0)
        x_all = post_block(x_all, o_all, om_all, w_out_bf, norm_ffn[l], w_up_bf, w_down_bf, l,
                           tm_ffn, 2048)

    def split(a, width_shape):
        a = jnp.stack(a)
        ap = a[:, :rp].reshape(a.shape[0], n, t, *width_shape)
        asmp = a[:, rp:rp + rs].reshape(a.shape[0], ns, SAMPLE_ROWS, *width_shape)[:, :, :ts]
        return ap, asmp

    y_prompt = x_all[:rp].reshape(n, t, d)
    y_sample = x_all[rp:rp + rs].reshape(ns, SAMPLE_ROWS, d)[:, :ts]
    k_p, k_s = split(new_k, (N_HEADS, HEAD_DIM))
    v_p, v_s = split(new_v, (N_HEADS, HEAD_DIM))
    idx_p, idx_s = split(new_idx, (IDX_DIM,))
    mem_k_out = mem_k_p.reshape(depth, n, mem_len, MEM_HEADS, HEAD_DIM)
    mem_v_out = mem_v_p.reshape(depth, n, mem_len, MEM_HEADS, HEAD_DIM)
    return (y_prompt, y_sample, k_p, v_p, idx_p, mem_k_out, mem_v_out,
            jnp.stack(new_delta_p), jnp.stack(new_conv_p), k_s, v_s, idx_s,
            jnp.stack(new_delta_s), jnp.stack(new_conv_s))
```

```python
import functools
import math

import jax
import jax.numpy as jnp
from jax import lax
from jax.experimental import pallas as pl
from jax.experimental.pallas import tpu as pltpu

F32 = jnp.float32
BF16 = jnp.bfloat16
I32 = jnp.int32
HI = lax.Precision.HIGHEST

D_MODEL = 1024
HEAD_DIM = 64
N_HEADS = 12
ATT_W = N_HEADS * HEAD_DIM
MEM_HEADS = 4
MEM_W = MEM_HEADS * HEAD_DIM
ROT_DIM = HEAD_DIM // 4
ROPE_THETA = 500000.0
ATT_SCALE = HEAD_DIM ** -0.5
LOG2E = math.log2(math.e)
MOBA_BLOCK = 256
MOBA_TOPK = 3
IDX_HEADS = 8
IDX_DIM = 32
IDX_ROT = IDX_DIM // 4
DSA_TOPK = 256
GDN_HEADS = 6
GDN_DK = 128
GDN_W = GDN_HEADS * GDN_DK
CONV_W = 4
CONV_CH = 3 * GDN_W
GDN_CHUNK = 64
D_FF = 4 * D_MODEL
EPS = 1e-6
NEG = -1e30
PAGE_SIZE = 128
N_MIXERS = 3

LANES = 128
SUBLANES = 8
BF16_ROWS = 16
ROW_TILE = 256
SAMPLE_ROWS = 8
VMEM_LIMIT = 56 * 1024 * 1024
INT_MIN = -2 ** 31
KEYS_PER_ROW = LANES // IDX_DIM
SUB_SPACING_STEPS = (0.5, 0.25, 0.125, 0.0625)
PAGES_PER_STEP = 8

MOBA_C = 3 * ATT_W + MEM_W
DSA_C = 3 * ATT_W + 2 * MEM_W + LANES
GDN_C = 4 * GDN_W + MEM_W + LANES


def _cparams(sem):
    return pltpu.CompilerParams(dimension_semantics=sem, vmem_limit_bytes=VMEM_LIMIT)


def _dot_nt(a, b, precision=None):
    return lax.dot_general(a, b, (((1,), (1,)), ((), ())), precision=precision,
                           preferred_element_type=F32)


def _dot3(a, b, dims):
    a_hi = a.astype(BF16)
    b_hi = b.astype(BF16)
    a_lo = (a - a_hi.astype(F32)).astype(BF16)
    b_lo = (b - b_hi.astype(F32)).astype(BF16)
    dg = lambda x, y: lax.dot_general(x, y, dims, preferred_element_type=F32)
    return dg(a_hi, b_hi) + dg(a_hi, b_lo) + dg(a_lo, b_hi)


def _head_block_ones(width):
    r = lax.broadcasted_iota(I32, (LANES, LANES), 0) // width
    c = lax.broadcasted_iota(I32, (LANES, LANES), 1) // width
    return jnp.where(r == c, 1.0, 0.0).astype(BF16)


def _group_sum(x2, gmat):
    hi = x2.astype(BF16)
    lo = (x2 - hi.astype(F32)).astype(BF16)
    return (jnp.dot(hi, gmat, preferred_element_type=F32)
            + jnp.dot(lo, gmat, preferred_element_type=F32))


def _head_rms(x, gain, gmat, width):
    ms = _group_sum(x * x, gmat) * (1.0 / width)
    return x * lax.rsqrt(ms + EPS) * gain


def _rope(y, c, sa, sb, half):
    return y * c + pltpu.roll(y, LANES - half, 1) * sa + pltpu.roll(y, half, 1) * sb


def _norm_matmul_kernel(x_ref, g_ref, w_ref, o_ref):
    x = x_ref[...]
    ms = jnp.mean(x * x, axis=-1, keepdims=True)
    h = (x * lax.rsqrt(ms + EPS) * g_ref[...]).astype(BF16)
    o_ref[...] = jnp.dot(h, w_ref[...], preferred_element_type=F32)


def norm_matmul(x, g, w_all, j):
    r, d = x.shape
    c = w_all.shape[2]
    tm = ROW_TILE
    return pl.pallas_call(
        _norm_matmul_kernel,
        grid=(r // tm,),
        in_specs=[pl.BlockSpec((tm, d), lambda i: (i, 0)),
                  pl.BlockSpec((1, d), lambda i: (0, 0)),
                  pl.BlockSpec((None, d, c), lambda i: (j, 0, 0))],
        out_specs=pl.BlockSpec((tm, c), lambda i: (i, 0)),
        out_shape=jax.ShapeDtypeStruct((r, c), F32),
        compiler_params=_cparams(("parallel",)),
        name="norm_matmul",
    )(x, g.reshape(1, d), w_all)


def _matmul_kernel(x_ref, w_ref, o_ref):
    o_ref[...] = jnp.dot(x_ref[...].astype(BF16), w_ref[...], preferred_element_type=F32)


def layer_matmul(x, w):
    r, d = x.shape
    nl, _, c = w.shape
    return pl.pallas_call(
        _matmul_kernel,
        grid=(nl,),
        in_specs=[pl.BlockSpec((r, d), lambda l: (0, 0)),
                  pl.BlockSpec((None, d, c), lambda l: (l, 0, 0))],
        out_specs=pl.BlockSpec((None, r, c), lambda l: (l, 0, 0)),
        out_shape=jax.ShapeDtypeStruct((nl, r, c), F32),
        compiler_params=_cparams(("parallel",)),
        name="mem_kv_matmul",
    )(x, w)


def _mem_k_norm_kernel(kv_ref, g_ref, o_ref):
    gmat = _head_block_ones(HEAD_DIM)
    g = g_ref[...]
    for p in range(MEM_W // LANES):
        x = kv_ref[:, p * LANES:(p + 1) * LANES]
        o_ref[:, p * LANES:(p + 1) * LANES] = _head_rms(x, g, gmat, HEAD_DIM)


def mem_key_norm(kv, gains):
    nl, r, _ = kv.shape
    return pl.pallas_call(
        _mem_k_norm_kernel,
        grid=(nl,),
        in_specs=[pl.BlockSpec((None, r, MEM_W), lambda l: (l, 0, 0)),
                  pl.BlockSpec((None, 1, LANES), lambda l: (l, 0, 0))],
        out_specs=pl.BlockSpec((None, r, MEM_W), lambda l: (l, 0, 0)),
        out_shape=jax.ShapeDtypeStruct((nl, r, MEM_W), F32),
        compiler_params=_cparams(("parallel",)),
        name="mem_k_norm",
    )(kv, gains)


def _qk_post_kernel(is_dsa, *refs):
    if is_dsa:
        (q_ref, k_ref, v_ref, qi_ref, kw_ref, c_ref, sa_ref, sb_ref, ci_ref, sai_ref, sbi_ref,
         qg_ref, kg_ref, qo_ref, ko_ref, km_ref, vo_ref, qio_ref, kit_ref) = refs
    else:
        (q_ref, k_ref, v_ref, c_ref, sa_ref, sb_ref, qg_ref, kg_ref,
         qo_ref, ko_ref, km_ref, vo_ref) = refs
    vo_ref[...] = v_ref[...]
    gmat = _head_block_ones(HEAD_DIM)
    c, sa, sb = c_ref[...], sa_ref[...], sb_ref[...]
    qg, kg = qg_ref[...], kg_ref[...]
    half = ROT_DIM // 2
    for p in range(ATT_W // LANES):
        sl = slice(p * LANES, (p + 1) * LANES)
        qn = _rope(_head_rms(q_ref[:, sl], qg, gmat, HEAD_DIM), c, sa, sb, half)
        qo_ref[:, sl] = qn * (ATT_SCALE * LOG2E)
        kn = _rope(_head_rms(k_ref[:, sl], kg, gmat, HEAD_DIM), c, sa, sb, half)
        ko_ref[:, sl] = kn
        km_ref[:, sl] = jnp.sum(kn, axis=0, keepdims=True) * (1.0 / ROW_TILE)
    if is_dsa:
        ci, sai, sbi = ci_ref[...], sai_ref[...], sbi_ref[...]
        ihalf = IDX_ROT // 2
        for p in range(IDX_HEADS * IDX_DIM // LANES):
            sl = slice(p * LANES, (p + 1) * LANES)
            qio_ref[:, sl] = _rope(qi_ref[:, sl], ci, sai, sbi, ihalf)
        kw = kw_ref[...]
        lane = lax.broadcasted_iota(I32, kw.shape, 1)
        kz = jnp.where(lane < IDX_DIM, kw, 0.0)
        kt = kz
        for s in range(1, LANES // IDX_DIM):
            kt = kt + pltpu.roll(kz, s * IDX_DIM, 1)
        kit_ref[...] = _rope(kt, ci, sai, sbi, ihalf)


def qk_post(proj, is_dsa, tabs, itabs, q_gain, k_gain):
    r = proj.shape[0]
    tm = ROW_TILE
    row = lambda cb: (lambda i: (i, cb))
    tab_spec = pl.BlockSpec((tm, LANES), lambda i: (i, 0))
    gain_spec = pl.BlockSpec((1, LANES), lambda i: (0, 0))
    in_specs = [pl.BlockSpec((tm, ATT_W), row(0)), pl.BlockSpec((tm, ATT_W), row(1)),
                pl.BlockSpec((tm, ATT_W), row(2))]
    args = [proj, proj, proj]
    if is_dsa:
        in_specs += [pl.BlockSpec((tm, MEM_W), row(3 * ATT_W // MEM_W)),
                     pl.BlockSpec((tm, LANES), row((3 * ATT_W + 2 * MEM_W) // LANES))]
        args += [proj, proj]
    in_specs += [tab_spec] * 3
    args += list(tabs)
    if is_dsa:
        in_specs += [tab_spec] * 3
        args += list(itabs)
    in_specs += [gain_spec, gain_spec]
    args += [q_gain, k_gain]
    out_specs = [pl.BlockSpec((tm, ATT_W), row(0)), pl.BlockSpec((tm, ATT_W), row(0)),
                 pl.BlockSpec((None, 1, ATT_W), lambda i: (i, 0, 0)),
                 pl.BlockSpec((tm, ATT_W), row(0))]
    out_shape = [jax.ShapeDtypeStruct((r, ATT_W), F32), jax.ShapeDtypeStruct((r, ATT_W), F32),
                 jax.ShapeDtypeStruct((r // tm, 1, ATT_W), F32),
                 jax.ShapeDtypeStruct((r, ATT_W), F32)]
    if is_dsa:
        out_specs += [pl.BlockSpec((tm, MEM_W), row(0)), pl.BlockSpec((tm, LANES), row(0))]
        out_shape += [jax.ShapeDtypeStruct((r, MEM_W), F32), jax.ShapeDtypeStruct((r, LANES), F32)]
    return pl.pallas_call(
        functools.partial(_qk_post_kernel, is_dsa),
        grid=(r // tm,),
        in_specs=in_specs, out_specs=out_specs, out_shape=out_shape,
        compiler_params=_cparams(("parallel",)),
        name="qk_post_dsa" if is_dsa else "qk_post",
    )(*args)


def _post_kernel(x_ref, o_ref, om_ref, wa_ref, wb_ref, g_ref, wu_ref, wd_ref, y_ref, h_sc):
    f = pl.program_id(1)

    @pl.when(f == 0)
    def _():
        x1 = (x_ref[...]
              + jnp.dot(o_ref[...].astype(BF16), wa_ref[...], preferred_element_type=F32)
              + jnp.dot(om_ref[...].astype(BF16), wb_ref[...], preferred_element_type=F32))
        ms = jnp.mean(x1 * x1, axis=-1, keepdims=True)
        h_sc[...] = (x1 * lax.rsqrt(ms + EPS) * g_ref[...]).astype(BF16)
        y_ref[...] = x1

    up = jnp.dot(h_sc[...], wu_ref[...], preferred_element_type=F32)
    act = jnp.square(jnp.maximum(up, 0.0)).astype(BF16)
    y_ref[...] += jnp.dot(act, wd_ref[...], preferred_element_type=F32)


def post_block(x, o, om, w_out, g, w_up, w_down, l, tm, tf):
    r, d = x.shape
    return pl.pallas_call(
        _post_kernel,
        grid=(r // tm, D_FF // tf),
        in_specs=[pl.BlockSpec((tm, d), lambda i, f: (i, 0)),
                  pl.BlockSpec((tm, ATT_W), lambda i, f: (i, 0)),
                  pl.BlockSpec((tm, MEM_W), lambda i, f: (i, 0)),
                  pl.BlockSpec((None, ATT_W, d), lambda i, f: (l, 0, 0)),
                  pl.BlockSpec((None, MEM_W, d), lambda i, f: (l, ATT_W // MEM_W, 0)),
                  pl.BlockSpec((1, d), lambda i, f: (0, 0)),
                  pl.BlockSpec((None, d, tf), lambda i, f: (l, 0, f)),
                  pl.BlockSpec((None, tf, d), lambda i, f: (l, f, 0))],
        out_specs=pl.BlockSpec((tm, d), lambda i, f: (i, 0)),
        out_shape=jax.ShapeDtypeStruct((r, d), F32),
        scratch_shapes=[pltpu.VMEM((tm, d), BF16)],
        compiler_params=_cparams(("parallel", "arbitrary")),
        name="out_proj_ffn",
    )(x, o, om, w_out, w_out, g.reshape(1, d), w_up, w_down)


def _topk_block_bias(gate, n_past, n_top):
    nbk = gate.shape[1]
    col = lax.broadcasted_iota(I32, gate.shape, 1)
    past = col < n_past
    g = jnp.where(past, gate, NEG)
    sel = jnp.zeros(gate.shape, F32)
    for _ in range(n_top):
        mx = jnp.max(g, axis=1, keepdims=True)
        idx = jnp.min(jnp.where(g == mx, col, nbk), axis=1, keepdims=True)
        pick = col == idx
        sel = jnp.where(pick, 1.0, sel)
        g = jnp.where(pick, -jnp.inf, g)
    return jnp.where(past, jnp.where(sel > 0.5, 0.0, NEG), NEG)


def _topk_block_bias_t(gate, n_past, n_top):
    nbk = gate.shape[0]
    row = lax.broadcasted_iota(I32, gate.shape, 0)
    past = row < n_past
    g = jnp.where(past, gate, NEG)
    sel = jnp.zeros(gate.shape, F32)
    for _ in range(n_top):
        mx = jnp.max(g, axis=0, keepdims=True)
        idx = jnp.min(jnp.where(g == mx, row, nbk), axis=0, keepdims=True)
        pick = row == idx
        sel = jnp.where(pick, 1.0, sel)
        g = jnp.where(pick, -jnp.inf, g)
    return jnp.where(past, jnp.where(sel > 0.5, 0.0, NEG), NEG)


def _prompt_attn_kernel(is_dsa, nb, n_top, q_ref, k_ref, v_ref, aux_ref, o_ref,
                        kbf_sc, vt_sc, m_sc, acc_sc, bias_sc):
    blk = MOBA_BLOCK
    i = pl.program_id(2)

    @pl.when(i == 0)
    def _():
        kbf_sc[...] = k_ref[...].astype(BF16)
        ones_row = jnp.where(lax.broadcasted_iota(I32, (BF16_ROWS, blk), 0) == 0, 1.0, 0.0)
        for c in range(nb):
            vt_sc[c, 0:LANES, :] = jnp.transpose(v_ref[c * blk:(c + 1) * blk, :]).astype(BF16)
            vt_sc[c, LANES:LANES + BF16_ROWS, :] = ones_row.astype(BF16)

    q = q_ref[...]
    lane = lax.broadcasted_iota(I32, q.shape, 1)
    qcat = jnp.concatenate([jnp.where(lane // HEAD_DIM == hh, q, 0.0) for hh in range(2)], axis=0)
    qcat_bf = qcat.astype(BF16)
    m_sc[...] = jnp.full(m_sc.shape, NEG, F32)
    acc_sc[...] = jnp.zeros(acc_sc.shape, F32)

    def scores(j):
        st = _dot_nt(kbf_sc[pl.ds(pl.multiple_of(j * blk, blk), blk), :], qcat_bf)
        if is_dsa:
            mb = aux_ref[j].astype(F32)
            st = st + jnp.concatenate([mb, mb], axis=1)
        return st

    def update(js, sts, col_bias):
        m_old = m_sc[...]
        m_new = m_old
        for st, cb in zip(sts, col_bias):
            mx = jnp.max(st, axis=0, keepdims=True)
            m_new = jnp.maximum(m_new, mx if cb is None else mx + cb)
        acc = jnp.exp2(m_old - m_new) * acc_sc[...]
        for j, st, cb in zip(js, sts, col_bias):
            p = jnp.exp2((st - (m_new if cb is None else m_new - cb)).astype(BF16))
            acc = acc + jnp.dot(vt_sc[j], p, preferred_element_type=F32)
        acc_sc[...] = acc
        m_sc[...] = m_new

    if is_dsa:
        col_bias_of = lambda j: None
        count = i + 1
    else:
        gate = _dot_nt(aux_ref[...], qcat, HI)
        bias_sc[...] = _topk_block_bias_t(gate, i, n_top)
        kr = lax.broadcasted_iota(I32, (blk, 2 * blk), 0)
        qc = lax.broadcasted_iota(I32, (blk, 2 * blk), 1) % blk
        update([i], [scores(i) + jnp.where(kr <= qc, 0.0, NEG)], [None])
        col_bias_of = lambda j: bias_sc[pl.ds(j, 1), :]
        count = i

    npairs = count // 2

    def pair(jj, sts):
        nxt = jnp.minimum(jj + 1, npairs - 1)
        nxt_sts = (scores(2 * nxt), scores(2 * nxt + 1))
        js = [2 * jj, 2 * jj + 1]
        update(js, sts, [col_bias_of(j) for j in js])
        return nxt_sts

    lax.fori_loop(0, npairs, pair, (scores(0), scores(1)))

    @pl.when(count % 2 == 1)
    def _():
        update([count - 1], [scores(count - 1)], [col_bias_of(count - 1)])

    acc = acc_sc[...]
    res = acc[0:LANES, :] / acc[LANES:LANES + 1, :]
    drow = lax.broadcasted_iota(I32, (LANES, blk), 0)
    o_t = jnp.where(drow // HEAD_DIM == 0, res[:, :blk], res[:, blk:])
    o_ref[...] = jnp.transpose(o_t)


def prompt_attention(q_all, k_all, proj, aux, is_dsa, n, t):
    blk = MOBA_BLOCK
    nb = t // blk
    npair = ATT_W // LANES
    n_top = min(MOBA_TOPK, nb)
    vcol = 2 * ATT_W // LANES
    if is_dsa:
        aux_spec = pl.BlockSpec((None, None, nb, blk, blk), lambda b, p, i: (b, i, 0, 0, 0))
    else:
        aux_spec = pl.BlockSpec((nb, LANES), lambda b, p, i: (b, p))
    return pl.pallas_call(
        functools.partial(_prompt_attn_kernel, is_dsa, nb, n_top),
        grid=(n, npair, nb),
        in_specs=[pl.BlockSpec((blk, LANES), lambda b, p, i: (b * nb + i, p)),
                  pl.BlockSpec((t, LANES), lambda b, p, i: (b, p)),
                  pl.BlockSpec((t, LANES), lambda b, p, i: (b, vcol + p)),
                  aux_spec],
        out_specs=pl.BlockSpec((blk, LANES), lambda b, p, i: (b * nb + i, p)),
        out_shape=jax.ShapeDtypeStruct((n * t, ATT_W), F32),
        scratch_shapes=[pltpu.VMEM((t, LANES), BF16), pltpu.VMEM((nb, LANES + BF16_ROWS, blk), BF16),
                        pltpu.VMEM((1, 2 * blk), F32),
                        pltpu.VMEM((LANES + BF16_ROWS, 2 * blk), F32), pltpu.VMEM((nb, 2 * blk), F32)],
        compiler_params=_cparams(("parallel", "parallel", "arbitrary")),
        name="prompt_attn_dsa" if is_dsa else "prompt_attn_moba",
    )(q_all, k_all, proj, aux)


def _key_to_float(k):
    b = jnp.where(k < 0, k ^ jnp.int32(0x7FFFFFFF), k)
    return lax.bitcast_convert_type(b, F32)


def _select_topk(count_ge, count_gt, count_tie_lt, k_sel, shape, idx_bits):
    def vbit(b, t):
        cand = t + lax.shift_left(jnp.int32(1), 31 - b)
        return jnp.where(count_ge(_key_to_float(cand)) >= k_sel, cand, t)

    vkey = lax.fori_loop(0, 32, vbit, jnp.full(shape, INT_MIN, I32))
    v = _key_to_float(vkey)
    ulp = _key_to_float(vkey + 1) - v
    for frac in SUB_SPACING_STEPS:
        cand = v + ulp * frac
        v = jnp.where(count_ge(cand) >= k_sel, cand, v)
    need = k_sel - count_gt(v)

    def cbit(b, c):
        cand = c + lax.shift_left(jnp.int32(1), idx_bits - 1 - b)
        return jnp.where(count_tie_lt(v, cand) <= need - 1, cand, c)

    all_ties_taken = jnp.all(count_ge(v) == k_sel)
    c = lax.cond(all_ties_taken,
                 lambda: jnp.full(shape, (1 << idx_bits) - 1, I32),
                 lambda: lax.fori_loop(0, idx_bits, cbit, jnp.zeros(shape, I32)))
    return v, c


def _dsa_select_kernel(nb, k_sel, idx_bits, qi_ref, kit_ref, w_ref, o_ref, sc_sc):
    blk = MOBA_BLOCK
    i = pl.program_id(1)
    w_t = jnp.transpose(w_ref[...])
    lane = lax.broadcasted_iota(I32, (blk, LANES), 1)
    qm, wrow = [], []
    for h in range(IDX_HEADS):
        seg = qi_ref[:, (h // 4) * LANES:(h // 4 + 1) * LANES]
        qm.append(jnp.where(lane // IDX_DIM == h % 4, seg, 0.0).astype(BF16))
        wrow.append(w_t[IDX_DIM + h:IDX_DIM + h + 1, :] * (1.0 / 16.0))
    kr = lax.broadcasted_iota(I32, (blk, blk), 0)
    qpos = i * blk + lax.broadcasted_iota(I32, (blk, blk), 1)

    def score_block(c, carry):
        off = pl.multiple_of(c * blk, blk)
        kt = kit_ref[pl.ds(off, blk), :].astype(BF16)
        acc = jnp.zeros((blk, blk), F32)
        for h in range(IDX_HEADS):
            acc = acc + jnp.maximum(_dot_nt(kt, qm[h]), 0.0) * wrow[h]
        sc_sc[c] = jnp.where(c * blk + kr <= qpos, acc + 0.0, -jnp.inf)
        return carry

    lax.fori_loop(0, i + 1, score_block, 0)

    def fold(x):
        return jnp.sum(x.reshape(blk // SUBLANES, SUBLANES, blk), axis=0)

    def count_ge(cand):
        def body(c, acc):
            return acc + fold(jnp.where(sc_sc[c] >= cand, 1, 0))

        acc = lax.fori_loop(0, i + 1, body, jnp.zeros((SUBLANES, blk), I32))
        return jnp.sum(acc, axis=0, keepdims=True)

    def count_gt(cand):
        def body(c, acc):
            return acc + fold(jnp.where(sc_sc[c] > cand, 1, 0))

        acc = lax.fori_loop(0, i + 1, body, jnp.zeros((SUBLANES, blk), I32))
        return jnp.sum(acc, axis=0, keepdims=True)

    def count_tie_lt(v, cut):
        def body(c, acc):
            hit = (sc_sc[c] == v) & (c * blk + kr < cut)
            return acc + fold(jnp.where(hit, 1, 0))

        acc = lax.fori_loop(0, i + 1, body, jnp.zeros((SUBLANES, blk), I32))
        return jnp.sum(acc, axis=0, keepdims=True)

    v, cut = _select_topk(count_ge, count_gt, count_tie_lt, k_sel, (1, blk), idx_bits)
    take_all = i * blk + lax.broadcasted_iota(I32, (1, blk), 1) + 1 <= k_sel

    def emit(c, carry):
        s = sc_sc[c]
        kpos = c * blk + kr
        bias = jnp.where(s > v, 0.0, jnp.where((s == v) & (kpos <= cut), 0.0, NEG))
        bias = jnp.where(take_all, 0.0, bias)
        o_ref[c] = jnp.where(kpos <= qpos, bias, NEG).astype(BF16)
        return carry

    lax.fori_loop(0, i + 1, emit, 0)

    def fill(c, carry):
        o_ref[c] = jnp.full((blk, blk), NEG, BF16)
        return carry

    lax.fori_loop(i + 1, nb, fill, 0)


def dsa_select_prompt(qi, kit, proj, n, t):
    blk = MOBA_BLOCK
    nb = t // blk
    k_sel = min(DSA_TOPK, t // 4)
    idx_bits = max(1, (t - 1).bit_length())
    wcol = (3 * ATT_W + 2 * MEM_W) // LANES
    return pl.pallas_call(
        functools.partial(_dsa_select_kernel, nb, k_sel, idx_bits),
        grid=(n, nb),
        in_specs=[pl.BlockSpec((blk, MEM_W), lambda b, i: (b * nb + i, 0)),
                  pl.BlockSpec((t, LANES), lambda b, i: (b, 0)),
                  pl.BlockSpec((blk, LANES), lambda b, i: (b * nb + i, wcol))],
        out_specs=pl.BlockSpec((None, None, nb, blk, blk), lambda b, i: (b, i, 0, 0, 0)),
        out_shape=jax.ShapeDtypeStruct((n, nb, nb, blk, blk), BF16),
        scratch_shapes=[pltpu.VMEM((nb, blk, blk), F32)],
        compiler_params=_cparams(("parallel", "arbitrary")),
        name="dsa_select_prompt",
    )(qi, kit, proj)


def _mem_attn_kernel(q_ref, g_ref, mk_ref, mv_ref, o_ref):
    gmat = _head_block_ones(HEAD_DIM)
    g = g_ref[...]
    for p in range(MEM_W // LANES):
        sl = slice(p * LANES, (p + 1) * LANES)
        q = _head_rms(q_ref[:, sl], g, gmat, HEAD_DIM) * ATT_SCALE
        lane = lax.broadcasted_iota(I32, q.shape, 1)
        kb = mk_ref[:, sl].astype(BF16)
        vb = mv_ref[:, sl].astype(BF16)
        outs = []
        for hh in range(2):
            qh = jnp.where(lane // HEAD_DIM == hh, q, 0.0).astype(BF16)
            s = _dot_nt(qh, kb)
            pexp = jnp.exp(s - jnp.max(s, axis=1, keepdims=True))
            den = jnp.sum(pexp, axis=1, keepdims=True)
            outs.append(jnp.dot(pexp.astype(BF16), vb, preferred_element_type=F32) / den)
        o_ref[:, sl] = jnp.where(lane // HEAD_DIM == 0, outs[0], outs[1])


def mem_attention(proj, qcol, gain, mk, mv, row0, n, t, tq):
    nq = t // tq
    rb0 = row0 // tq
    ml = mk.shape[1]
    return pl.pallas_call(
        _mem_attn_kernel,
        grid=(n, nq),
        in_specs=[pl.BlockSpec((tq, MEM_W), lambda b, i: (rb0 + b * nq + i, qcol)),
                  pl.BlockSpec((1, LANES), lambda b, i: (0, 0)),
                  pl.BlockSpec((None, ml, MEM_W), lambda b, i: (b, 0, 0)),
                  pl.BlockSpec((None, ml, MEM_W), lambda b, i: (b, 0, 0))],
        out_specs=pl.BlockSpec((tq, MEM_W), lambda b, i: (b * nq + i, 0)),
        out_shape=jax.ShapeDtypeStruct((n * t, MEM_W), F32),
        compiler_params=_cparams(("parallel", "parallel")),
        name="mem_attn",
    )(proj, gain, mk, mv)


def _tile_rows(x, reps):
    return jnp.concatenate([x] * reps, axis=0)


def _block_diag_queries(q8):
    qt = _tile_rows(q8, N_HEADS)
    row = lax.broadcasted_iota(I32, qt.shape, 0)
    lane = lax.broadcasted_iota(I32, qt.shape, 1)
    return jnp.where(lane // HEAD_DIM == row // SAMPLE_ROWS, qt, 0.0)


def _sample_attn_kernel(nsteps, nbp, n_top, use_gate, *refs):
    npp = PAGES_PER_STEP
    pt_ref = refs[0]
    k_refs, v_refs = refs[1:1 + npp], refs[1 + npp:1 + 2 * npp]
    (q_ref, kn_ref, vn_ref, km_ref, nbias_ref, o_ref,
     qbd_sc, kmean_sc, m_sc, l_sc, acc_sc) = refs[1 + 2 * npp:]
    j = pl.program_id(1)
    ppb = MOBA_BLOCK // PAGE_SIZE

    @pl.when(j == 0)
    def _():
        qbd_sc[...] = _block_diag_queries(q_ref[...])
        kmean_sc[...] = jnp.zeros(kmean_sc.shape, F32)

    qbd = qbd_sc[...]
    qbd_bf = qbd.astype(BF16)
    kmask = km_ref[...]
    ones = jnp.ones((PAGE_SIZE, LANES), BF16)
    lane = lax.broadcasted_iota(I32, (ATT_W, LANES), 1)
    for blk in range(npp // ppb):
        bi = (npp // ppb) * j + blk
        scores, ksum = [], jnp.zeros((ATT_W, LANES), F32)
        for pg in range(ppb):
            kf = k_refs[blk * ppb + pg][...]
            khi = kf.astype(BF16)
            scores.append(jnp.dot(qbd_bf, khi, preferred_element_type=F32))
            if use_gate:
                klo = (kf - khi.astype(F32)).astype(BF16)
                ksum = (ksum + jnp.dot(khi, ones, preferred_element_type=F32)
                        + jnp.dot(klo, ones, preferred_element_type=F32))
        if use_gate:
            kmean_sc[...] += jnp.where(lane == bi, ksum * (1.0 / MOBA_BLOCK), 0.0)
        s = jnp.concatenate(scores, axis=1)
        s = s + _tile_rows(kmask[:, blk * MOBA_BLOCK:(blk + 1) * MOBA_BLOCK], N_HEADS)
        m = jnp.max(s, axis=1, keepdims=True)
        p = jnp.exp2(s - m)
        pb = p.astype(BF16)
        acc = jnp.zeros((N_HEADS * SAMPLE_ROWS, ATT_W), F32)
        for pg in range(ppb):
            vb = v_refs[blk * ppb + pg][...].astype(BF16)
            acc = acc + _dot_nt(pb[:, pg * PAGE_SIZE:(pg + 1) * PAGE_SIZE], vb)
        m_sc[bi] = m
        l_sc[bi] = jnp.sum(p, axis=1, keepdims=True)
        acc_sc[bi] = acc

    @pl.when(j == nsteps - 1)
    def _():
        pad = jnp.zeros((PAGE_SIZE - SAMPLE_ROWS, ATT_W), F32)
        kn = jnp.concatenate([kn_ref[...], pad], axis=0).astype(BF16)
        vn = jnp.concatenate([vn_ref[...], pad], axis=0).astype(BF16)
        s = _dot_nt(qbd_bf, kn) + _tile_rows(nbias_ref[...], N_HEADS)
        m_n = jnp.max(s, axis=1, keepdims=True)
        p = jnp.exp2(s - m_n)
        l_n = jnp.sum(p, axis=1, keepdims=True)
        acc_n = jnp.dot(p.astype(BF16), vn, preferred_element_type=F32)
        if use_gate:
            gate = jnp.dot(qbd, kmean_sc[...], precision=HI, preferred_element_type=F32)
            bias = _topk_block_bias(gate, nbp, n_top)
            mb = [m_sc[b] + bias[:, b:b + 1] for b in range(nbp)]
        else:
            mb = [m_sc[b] for b in range(nbp)]
        m_all = m_n
        for b in range(nbp):
            m_all = jnp.maximum(m_all, mb[b])
        w_n = jnp.exp2(m_n - m_all)
        den = w_n * l_n
        num = w_n * acc_n
        for b in range(nbp):
            w_b = jnp.exp2(mb[b] - m_all)
            den = den + w_b * l_sc[b]
            num = num + w_b * acc_sc[b]
        res = num / den
        olane = lax.broadcasted_iota(I32, (SAMPLE_ROWS, ATT_W), 1)
        out = jnp.zeros((SAMPLE_ROWS, ATT_W), F32)
        for h in range(N_HEADS):
            out = out + jnp.where(olane // HEAD_DIM == h,
                                  res[h * SAMPLE_ROWS:(h + 1) * SAMPLE_ROWS, :], 0.0)
        o_ref[...] = out


def sample_attention(kt_all, vt_all, slot, page_table, q_all, k_all, proj, key_bias, new_bias,
                     use_gate, row0, ns):
    npg = page_table.shape[1]
    npp = PAGES_PER_STEP
    assert npg % npp == 0
    nsteps = npg // npp
    nbp = npg // (MOBA_BLOCK // PAGE_SIZE)
    assert nbp <= LANES
    n_top = min(MOBA_TOPK, nbp + 1)
    rb0 = row0 // SAMPLE_ROWS
    rows = N_HEADS * SAMPLE_ROWS
    page = lambda off: pl.BlockSpec((None, None, ATT_W, PAGE_SIZE),
                                    lambda b, j, pt: (slot, pt[b, npp * j + off], 0, 0))
    rowblk = lambda cb: pl.BlockSpec((SAMPLE_ROWS, ATT_W), lambda b, j, pt: (rb0 + b, cb))
    grid_spec = pltpu.PrefetchScalarGridSpec(
        num_scalar_prefetch=1,
        grid=(ns, nsteps),
        in_specs=[page(off) for off in range(npp)] * 2 + [
            rowblk(0), rowblk(0), rowblk(2),
            pl.BlockSpec((None, SAMPLE_ROWS, npp * PAGE_SIZE), lambda b, j, pt: (b, 0, j)),
            pl.BlockSpec((None, SAMPLE_ROWS, LANES), lambda b, j, pt: (b, 0, 0))],
        out_specs=pl.BlockSpec((SAMPLE_ROWS, ATT_W), lambda b, j, pt: (b, 0)),
        scratch_shapes=[pltpu.VMEM((rows, ATT_W), F32), pltpu.VMEM((ATT_W, LANES), F32),
                        pltpu.VMEM((nbp, rows, 1), F32), pltpu.VMEM((nbp, rows, 1), F32),
                        pltpu.VMEM((nbp, rows, ATT_W), F32)],
    )
    return pl.pallas_call(
        functools.partial(_sample_attn_kernel, nsteps, nbp, n_top, use_gate),
        grid_spec=grid_spec,
        out_shape=jax.ShapeDtypeStruct((ns * SAMPLE_ROWS, ATT_W), F32),
        compiler_params=_cparams(("parallel", "arbitrary")),
        name="sample_attn_moba" if use_gate else "sample_attn_dsa",
    )(page_table, *([kt_all] * npp), *([vt_all] * npp), q_all, k_all, proj, key_bias, new_bias)


def _dsa_select_sample_kernel(npg, ts, k_sel, idx_bits, pt_ref, cidx_ref, qi_ref, kit_ref, w_ref,
                              km_ref, nb_ref, kbuf, sem):
    b = pl.program_id(0)
    past = npg * PAGE_SIZE
    kpr = PAGE_SIZE // KEYS_PER_ROW
    nrow = past // KEYS_PER_ROW

    def page_copy(p):
        return pltpu.make_async_copy(cidx_ref.at[pt_ref[b, p]], kbuf.at[pl.ds(p * kpr, kpr)], sem)

    for p in range(npg):
        page_copy(p).start()
    for p in range(npg):
        page_copy(p).wait()

    lane = lax.broadcasted_iota(I32, (SAMPLE_ROWS, LANES), 1)
    wblk = w_ref[...]
    kpast = kbuf[...].astype(BF16)
    knew = jnp.where(lane < IDX_DIM, kit_ref[...], 0.0)
    knew = jnp.concatenate([knew, jnp.zeros((LANES - SAMPLE_ROWS, LANES), F32)], axis=0).astype(BF16)
    isc_p = [jnp.zeros((SAMPLE_ROWS, nrow), F32) for _ in range(KEYS_PER_ROW)]
    isc_n = jnp.zeros((SAMPLE_ROWS, LANES), F32)
    for h in range(IDX_HEADS):
        seg = qi_ref[:, (h // 4) * LANES:(h // 4 + 1) * LANES]
        w_h = jnp.sum(jnp.where(lane == IDX_DIM + h, wblk, 0.0), axis=1, keepdims=True) * (1.0 / 16.0)
        for c in range(KEYS_PER_ROW):
            shift = ((c - h % 4) * IDX_DIM) % LANES
            qs = pltpu.roll(seg, shift, 1) if shift else seg
            qmh = jnp.where(lane // IDX_DIM == c, qs, 0.0).astype(BF16)
            isc_p[c] = isc_p[c] + jnp.maximum(_dot_nt(qmh, kpast), 0.0) * w_h
            if c == 0:
                isc_n = isc_n + jnp.maximum(_dot_nt(qmh, knew), 0.0) * w_h
    key_p = [x + 0.0 for x in isc_p]
    trow = lax.broadcasted_iota(I32, (SAMPLE_ROWS, LANES), 0)
    valid_n = (lane <= trow) & (lane < ts)
    key_n = jnp.where(valid_n, isc_n + 0.0, -jnp.inf)
    rid = lax.broadcasted_iota(I32, (SAMPLE_ROWS, nrow), 1)
    idx_p = [KEYS_PER_ROW * rid + c for c in range(KEYS_PER_ROW)]
    idx_n = past + lane
    cnt = lambda m: jnp.sum(jnp.where(m, 1, 0), axis=1, keepdims=True)

    def count_ge(cand):
        return sum([cnt(k >= cand) for k in key_p], cnt(key_n >= cand))

    def count_gt(cand):
        return sum([cnt(k > cand) for k in key_p], cnt(key_n > cand))

    def count_tie_lt(v, cut):
        return sum([cnt((k == v) & (ix < cut)) for k, ix in zip(key_p, idx_p)],
                   cnt((key_n == v) & (idx_n < cut)))

    v, cut = _select_topk(count_ge, count_gt, count_tie_lt, k_sel, (SAMPLE_ROWS, 1), idx_bits)
    trow1 = lax.broadcasted_iota(I32, (SAMPLE_ROWS, 1), 0)
    take_all = past + jnp.minimum(trow1, ts - 1) + 1 <= k_sel
    for c in range(KEYS_PER_ROW):
        sel = (key_p[c] > v) | ((key_p[c] == v) & (idx_p[c] <= cut))
        km_ref[c] = jnp.where(take_all, 0.0, jnp.where(sel, 0.0, NEG))
    sel_n = (key_n > v) | ((key_n == v) & (idx_n <= cut))
    nb_ref[...] = jnp.where(valid_n, jnp.where(take_all, 0.0, jnp.where(sel_n, 0.0, NEG)), NEG)


def dsa_select_sample(cache_idx_slot, page_table, qi, kit, proj, row0, ns, ts):
    npg = page_table.shape[1]
    past = npg * PAGE_SIZE
    k_sel = min(DSA_TOPK, (past + ts) // 4)
    idx_bits = max(1, (past + LANES - 1).bit_length())
    rb0 = row0 // SAMPLE_ROWS
    wcol = (3 * ATT_W + 2 * MEM_W) // LANES
    nrow = past // KEYS_PER_ROW
    pages = cache_idx_slot.reshape(-1, PAGE_SIZE // KEYS_PER_ROW, LANES)
    grid_spec = pltpu.PrefetchScalarGridSpec(
        num_scalar_prefetch=1,
        grid=(ns,),
        in_specs=[pl.BlockSpec(memory_space=pl.ANY),
                  pl.BlockSpec((SAMPLE_ROWS, MEM_W), lambda b, pt: (rb0 + b, 0)),
                  pl.BlockSpec((SAMPLE_ROWS, LANES), lambda b, pt: (rb0 + b, 0)),
                  pl.BlockSpec((SAMPLE_ROWS, LANES), lambda b, pt: (rb0 + b, wcol))],
        out_specs=[pl.BlockSpec((None, KEYS_PER_ROW, SAMPLE_ROWS, nrow), lambda b, pt: (b, 0, 0, 0)),
                   pl.BlockSpec((None, SAMPLE_ROWS, LANES), lambda b, pt: (b, 0, 0))],
        scratch_shapes=[pltpu.VMEM((nrow, LANES), F32), pltpu.SemaphoreType.DMA(())],
    )
    km, nbias = pl.pallas_call(
        functools.partial(_dsa_select_sample_kernel, npg, ts, k_sel, idx_bits),
        grid_spec=grid_spec,
        out_shape=[jax.ShapeDtypeStruct((ns, KEYS_PER_ROW, SAMPLE_ROWS, nrow), F32),
                   jax.ShapeDtypeStruct((ns, SAMPLE_ROWS, LANES), F32)],
        compiler_params=_cparams(("arbitrary",)),
        name="dsa_select_sample",
    )(page_table, pages, qi, kit, proj)
    return jnp.transpose(km, (0, 2, 3, 1)).reshape(ns, SAMPLE_ROWS, past), nbias


def _gdn_local_kernel(tc, cs, hps, t_valid, q_ref, k_ref, v_ref, ba_ref, cwq_ref, cwk_ref, cwv_ref,
                      cbq_ref, cbk_ref, cbv_ref, hp_ref, u_ref, w_ref, qd_ref, kd_ref, at_ref, xp_sc):
    h0 = pl.program_id(1) * hps
    c = pl.program_id(2)
    kw = CONV_W - 1

    @pl.when(c == 0)
    def _():
        for seg, cb in enumerate((cbq_ref, cbk_ref, cbv_ref)):
            xp_sc[seg, SUBLANES - kw:SUBLANES, :] = cb[...]

    acts = []
    for seg, (xr, cw) in enumerate(((q_ref, cwq_ref), (k_ref, cwk_ref), (v_ref, cwv_ref))):
        xp_sc[seg, SUBLANES:SUBLANES + tc, :] = xr[...]
        y = xp_sc[seg, SUBLANES - kw:SUBLANES - kw + tc, :] * cw[0:1, :]
        for jj in range(1, CONV_W):
            y = y + xp_sc[seg, SUBLANES - kw + jj:SUBLANES - kw + jj + tc, :] * cw[jj:jj + 1, :]
        acts.append(jax.nn.silu(y))
        xp_sc[seg, SUBLANES - kw:SUBLANES, :] = xp_sc[seg, SUBLANES + tc - kw:SUBLANES + tc, :]
    lane = lax.broadcasted_iota(I32, (tc, LANES), 1)
    lane1 = lax.broadcasted_iota(I32, (1, LANES), 1)
    ba = ba_ref[...]
    hp = hp_ref[...]
    qn, kn, va, beta, g = [], [], [], [], []
    for hh in range(hps):
        cl = slice(hh * LANES, (hh + 1) * LANES)
        h = h0 + hh
        qa, ka = acts[0][:, cl], acts[1][:, cl]
        qn.append(qa * lax.rsqrt(jnp.sum(qa * qa, axis=-1, keepdims=True) + EPS) * (GDN_DK ** -0.5))
        kn.append(ka * lax.rsqrt(jnp.sum(ka * ka, axis=-1, keepdims=True) + EPS))
        va.append(acts[2][:, cl])
        bcol = jnp.sum(jnp.where(lane == h, ba, 0.0), axis=1, keepdims=True)
        acol = jnp.sum(jnp.where(lane == GDN_HEADS + h, ba, 0.0), axis=1, keepdims=True)
        alog = jnp.sum(jnp.where(lane1 == h, hp[0:1, :], 0.0), axis=1, keepdims=True)
        dtb = jnp.sum(jnp.where(lane1 == h, hp[1:2, :], 0.0), axis=1, keepdims=True)
        beta_h = jax.nn.sigmoid(bcol)
        g_h = -jnp.exp(alog) * jax.nn.softplus(acol + dtb)
        if t_valid is not None:
            rowid = c * tc + lax.broadcasted_iota(I32, (tc, 1), 0)
            beta_h = jnp.where(rowid < t_valid, beta_h, 0.0)
            g_h = jnp.where(rowid < t_valid, g_h, 0.0)
        beta.append(beta_h)
        g.append(g_h)

    ri = lax.broadcasted_iota(I32, (cs, cs), 0)
    ci = lax.broadcasted_iota(I32, (cs, cs), 1)
    tri = ri >= ci
    stri = ri > ci
    tril = jnp.where(tri, 1.0, 0.0)
    eye = jnp.where(ri == ci, 1.0, 0.0)
    e0 = jnp.where(lax.broadcasted_iota(I32, (cs, LANES), 1) == 0, 1.0, 0.0)
    mm = lambda a, b: _dot3(a, b, (((1,), (0,)), ((), ())))
    mm_nt = lambda a, b: _dot3(a, b, (((1,), (1,)), ((), ())))
    n_sq = max(0, (cs - 1).bit_length() - 1)

    units = [(hh, slice(ch * cs, (ch + 1) * cs)) for hh in range(hps) for ch in range(tc // cs)]
    gcum = [jnp.dot(tril, jnp.broadcast_to(g[hh][rs], (cs, LANES)), precision=HI,
                    preferred_element_type=F32) for hh, rs in units]
    grow = [_dot_nt(e0, x, HI) for x in gcum]
    decay = [jnp.where(tri, jnp.exp(jnp.where(tri, gc[:, 0:cs] - gr, 0.0)), 0.0)
             for gc, gr in zip(gcum, grow)]
    kb = [kn[hh][rs] * beta[hh][rs] for hh, rs in units]
    pw = [-jnp.where(stri, mm_nt(kbc, kn[hh][rs]) * dc, 0.0)
          for kbc, (hh, rs), dc in zip(kb, units, decay)]
    tinv = [eye + p for p in pw]
    for _ in range(n_sq):
        pw = [mm(p, p) for p in pw]
        tinv = [t + mm(t, p) for t, p in zip(tinv, pw)]
    egc = [jnp.exp(x) for x in gcum]
    for i, (hh, rs) in enumerate(units):
        cl = slice(hh * LANES, (hh + 1) * LANES)
        glast = gcum[i][cs - 1:cs, :]
        u_ref[rs, cl] = mm(tinv[i], va[hh][rs] * beta[hh][rs])
        w_ref[rs, cl] = mm(tinv[i], kb[i] * egc[i])
        qd_ref[rs, cl] = qn[hh][rs] * egc[i]
        kd_ref[rs, cl] = kn[hh][rs] * jnp.exp(glast - gcum[i])
        at = jnp.where(tri, mm_nt(qn[hh][rs], kn[hh][rs]) * decay[i], 0.0)
        at_ref[rs, cl] = jnp.zeros((cs, LANES), F32)
        at_ref[rs, hh * LANES:hh * LANES + cs] = at
        at_ref[rs, hh * LANES + LANES // 2:(hh + 1) * LANES] = jnp.broadcast_to(
            jnp.exp(glast), (cs, LANES))[:, LANES // 2:]


def _gdn_scan_kernel(tc, cs, u_ref, w_ref, qd_ref, kd_ref, at_ref, z_ref, on_ref, s0_ref,
                     o_ref, so_ref, s_sc):
    c = pl.program_id(1)
    nc = pl.num_programs(1)

    @pl.when(c == 0)
    def _():
        s_sc[...] = s0_ref[...]

    mm = lambda a, b: _dot3(a, b, (((1,), (0,)), ((), ())))
    lane = lax.broadcasted_iota(I32, (cs, LANES), 1)
    on = on_ref[...]
    heads = [slice(h * LANES, (h + 1) * LANES) for h in range(GDN_HEADS)]
    state = [s_sc[h] for h in range(GDN_HEADS)]
    for ch in range(tc // cs):
        rs = slice(ch * cs, (ch + 1) * cs)
        at = [at_ref[rs, cl] for cl in heads]
        v_new = [u_ref[rs, cl] - mm(w_ref[rs, cl], s) for cl, s in zip(heads, state)]
        o = [mm(qd_ref[rs, cl], s) + mm(a[:, 0:cs], vn)
             for cl, s, a, vn in zip(heads, state, at, v_new)]
        decay_tot = [jnp.where(lane < LANES // 2, pltpu.roll(a, LANES // 2, 1), a)[0:1, :] for a in at]
        state = [s * dt + _dot3(kd_ref[rs, cl], vn, (((0,), (0,)), ((), ())))
                 for cl, s, dt, vn in zip(heads, state, decay_tot, v_new)]
        for cl, oh in zip(heads, o):
            ms = jnp.mean(oh * oh, axis=-1, keepdims=True)
            o_ref[rs, cl] = oh * lax.rsqrt(ms + EPS) * on * jax.nn.silu(z_ref[rs, cl])
    for h in range(GDN_HEADS):
        s_sc[h] = state[h]

    @pl.when(c == nc - 1)
    def _():
        so_ref[...] = s_sc[...]


def gdn_mix(proj, conv_w, conv_buf, head_params, out_norm, s0, row0, n, t, tc, cs, hps, t_valid):
    nc = t // tc
    rb0 = row0 // tc
    bacol = (4 * GDN_W + MEM_W) // LANES
    kw = CONV_W - 1
    assert cs <= LANES // 2 and GDN_HEADS % hps == 0
    hw = hps * LANES
    hb = GDN_W // hw
    seg_spec = lambda s: pl.BlockSpec((tc, hw), lambda b, h, c: (rb0 + b * nc + c, s * hb + h))
    cw_spec = lambda s: pl.BlockSpec((CONV_W, hw), lambda b, h, c: (0, s * hb + h))
    cb_spec = lambda s: pl.BlockSpec((None, kw, hw), lambda b, h, c: (b, 0, s * hb + h))
    loc_spec = pl.BlockSpec((tc, hw), lambda b, h, c: (b * nc + c, h))
    loc_shape = jax.ShapeDtypeStruct((n * t, GDN_W), F32)
    local = pl.pallas_call(
        functools.partial(_gdn_local_kernel, tc, cs, hps, t_valid),
        grid=(n, hb, nc),
        in_specs=[seg_spec(0), seg_spec(1), seg_spec(2),
                  pl.BlockSpec((tc, LANES), lambda b, h, c: (rb0 + b * nc + c, bacol)),
                  cw_spec(0), cw_spec(1), cw_spec(2), cb_spec(0), cb_spec(1), cb_spec(2),
                  pl.BlockSpec((2, LANES), lambda b, h, c: (0, 0))],
        out_specs=[loc_spec] * 5,
        out_shape=[loc_shape] * 5,
        scratch_shapes=[pltpu.VMEM((3, SUBLANES + tc, hw), F32)],
        compiler_params=_cparams(("parallel", "parallel", "arbitrary")),
        name="gdn_local",
    )(proj, proj, proj, proj, conv_w, conv_w, conv_w, conv_buf, conv_buf, conv_buf, head_params)
    wide = pl.BlockSpec((tc, GDN_W), lambda b, c: (b * nc + c, 0))
    state_spec = pl.BlockSpec((None, GDN_HEADS, GDN_DK, LANES), lambda b, c: (b, 0, 0, 0))
    return pl.pallas_call(
        functools.partial(_gdn_scan_kernel, tc, cs),
        grid=(n, nc),
        in_specs=[wide] * 5 + [
            pl.BlockSpec((tc, GDN_W), lambda b, c: (rb0 + b * nc + c, 3)),
            pl.BlockSpec((1, LANES), lambda b, c: (0, 0)),
            state_spec],
        out_specs=[wide, state_spec],
        out_shape=[loc_shape, jax.ShapeDtypeStruct((n, GDN_HEADS, GDN_DK, LANES), F32)],
        scratch_shapes=[pltpu.VMEM((GDN_HEADS, GDN_DK, LANES), F32)],
        compiler_params=_cparams(("parallel", "arbitrary")),
        name="gdn_scan",
    )(*local, proj, out_norm, s0)


def _rope_tables(pos, rot_dim, head_dim):
    half = rot_dim // 2
    inv_freq = ROPE_THETA ** (-jnp.arange(0, rot_dim, 2, dtype=F32) / rot_dim)
    ang = pos.astype(F32)[:, None] * inv_freq[None, :]
    cos, sin = jnp.cos(ang), jnp.sin(ang)
    r = pos.shape[0]
    rest = head_dim - 2 * half
    zh = jnp.zeros((r, half), F32)
    c = jnp.concatenate([cos, cos, jnp.ones((r, rest), F32)], axis=1)
    sa = jnp.concatenate([-sin, zh, jnp.zeros((r, rest), F32)], axis=1)
    sb = jnp.concatenate([zh, sin, jnp.zeros((r, rest), F32)], axis=1)
    reps = LANES // head_dim
    return tuple(jnp.tile(a, (1, reps)) for a in (c, sa, sb))


def _lane_gain(g):
    return jnp.tile(g.astype(F32), LANES // g.shape[0]).reshape(1, LANES)


def _pick_tile(r, options):
    for tm in options:
        if r % tm == 0:
            return tm
    raise ValueError(f"no row tile for {r}")


def kernel(x_prompt, x_sample, mem_prompt, cache_k, cache_v, cache_idx, cache_mem_k, cache_mem_v,
           state_delta, state_conv, page_table, norm_mix, norm_ffn, attn_q_norm, attn_k_norm,
           mem_q_norm, mem_k_norm, w_in_moba, w_in_dsa, w_in_gdn, gdn_conv, gdn_a_log, gdn_dt_bias,
           gdn_out_norm, w_mem_kv, w_out, w_up, w_down):
    n, t, d = x_prompt.shape
    ns, ts, _ = x_sample.shape
    depth = norm_mix.shape[0]
    npg = page_table.shape[1]
    past = npg * PAGE_SIZE
    mem_len = mem_prompt.shape[1]
    assert d == D_MODEL and t % MOBA_BLOCK == 0 and t >= CONV_W - 1
    assert CONV_W - 1 <= ts <= SAMPLE_ROWS and past % MOBA_BLOCK == 0
    rp = n * t
    rs = ns * SAMPLE_ROWS
    r = -(-(rp + rs) // ROW_TILE) * ROW_TILE
    page_table = page_table.astype(I32)

    xs = jnp.pad(x_sample, ((0, 0), (0, SAMPLE_ROWS - ts), (0, 0))).reshape(rs, d)
    x_all = jnp.concatenate([x_prompt.reshape(rp, d), xs, jnp.zeros((r - rp - rs, d), F32)], axis=0)
    pos_s = past + jnp.minimum(jnp.arange(SAMPLE_ROWS, dtype=I32), ts - 1)
    pos_all = jnp.concatenate([jnp.tile(jnp.arange(t, dtype=I32), n), jnp.tile(pos_s, ns),
                               jnp.zeros((r - rp - rs,), I32)])
    tabs = _rope_tables(pos_all, ROT_DIM, HEAD_DIM)
    itabs = _rope_tables(pos_all, IDX_ROT, IDX_DIM)

    zpad = lambda w, c: jnp.pad(w, ((0, 0), (0, 0), (0, c - w.shape[-1])))
    a3 = 3 * ATT_W
    qi_w = IDX_HEADS * IDX_DIM
    kw_w = IDX_DIM + IDX_HEADS
    w_moba = w_in_moba.astype(BF16)
    w_dsa = zpad(jnp.concatenate([w_in_dsa[..., :a3 + qi_w], w_in_dsa[..., a3 + qi_w + kw_w:],
                                  w_in_dsa[..., a3 + qi_w:a3 + qi_w + kw_w]], axis=-1), DSA_C).astype(BF16)
    g4 = 4 * GDN_W
    w_gdn = zpad(jnp.concatenate([w_in_gdn[..., :g4], w_in_gdn[..., g4 + 2 * GDN_HEADS:],
                                  w_in_gdn[..., g4:g4 + 2 * GDN_HEADS]], axis=-1), GDN_C).astype(BF16)
    w_out_bf = w_out.astype(BF16)
    w_up_bf = w_up.astype(BF16)
    w_down_bf = w_down.astype(BF16)

    kv = layer_matmul(mem_prompt.reshape(n * mem_len, d), w_mem_kv.astype(BF16))
    mk_gain = jnp.tile(mem_k_norm.astype(F32), (1, LANES // HEAD_DIM)).reshape(depth, 1, LANES)
    mem_k_p = mem_key_norm(kv, mk_gain).reshape(depth, n, mem_len, MEM_W)
    mem_v_p = kv[..., MEM_W:].reshape(depth, n, mem_len, MEM_W)
    mem_k_s = cache_mem_k.reshape(depth, ns, mem_len, MEM_W)
    mem_v_s = cache_mem_v.reshape(depth, ns, mem_len, MEM_W)

    tm_ffn = _pick_tile(r, (640, 512, 256))
    tq_mem = _pick_tile(t, (512, 256))
    causal_new = jnp.where((jnp.arange(LANES)[None, :] <= jnp.arange(SAMPLE_ROWS)[:, None])
                           & (jnp.arange(LANES)[None, :] < ts), 0.0, NEG).astype(F32)
    causal_new = jnp.broadcast_to(causal_new, (ns, SAMPLE_ROWS, LANES))
    zero_key_bias = jnp.zeros((ns, SAMPLE_ROWS, past), F32)
    kt_all = jnp.transpose(cache_k, (0, 1, 3, 4, 2)).reshape(cache_k.shape[0], -1, ATT_W, PAGE_SIZE)
    vt_all = jnp.transpose(cache_v, (0, 1, 3, 4, 2)).reshape(cache_v.shape[0], -1, ATT_W, PAGE_SIZE)

    new_k, new_v, new_idx, new_delta_p, new_delta_s, new_conv_p, new_conv_s = [], [], [], [], [], [], []
    kind_count = [0, 0, 0]
    kv_slot = 0
    for l in range(depth):
        kind = l % N_MIXERS
        j = kind_count[kind]
        kind_count[kind] += 1
        proj = norm_matmul(x_all, norm_mix[l], (w_moba, w_dsa, w_gdn)[kind], j)
        if kind == 2:
            memq_col = 4 * GDN_W // MEM_W
            hp = jnp.zeros((2, LANES), F32)
            hp = hp.at[0, :GDN_HEADS].set(gdn_a_log[j].astype(F32)).at[1, :GDN_HEADS].set(
                gdn_dt_bias[j].astype(F32))
            on = gdn_out_norm[j].astype(F32).reshape(1, LANES)
            kwid = CONV_W - 1
            o_p, sd_p = gdn_mix(proj, gdn_conv[j], jnp.zeros((n, kwid, CONV_CH), F32), hp, on,
                                jnp.zeros((n, GDN_HEADS, GDN_DK, LANES), F32), 0, n, t,
                                MOBA_BLOCK, GDN_CHUNK, 3, None)
            o_s, sd_s = gdn_mix(proj, gdn_conv[j], state_conv[j], hp, on, state_delta[j].astype(F32),
                                rp, ns, SAMPLE_ROWS, SAMPLE_ROWS, SAMPLE_ROWS, GDN_HEADS, ts)
            new_delta_p.append(sd_p)
            new_delta_s.append(sd_s)
            new_conv_p.append(jnp.stack([proj[b * t + t - kwid:(b + 1) * t, :CONV_CH] for b in range(n)]))
            new_conv_s.append(proj[rp:rp + rs].reshape(ns, SAMPLE_ROWS, -1)[:, ts - kwid:ts, :CONV_CH])
        else:
            is_dsa = kind == 1
            memq_col = (3 * ATT_W + (MEM_W if is_dsa else 0)) // MEM_W
            outs = qk_post(proj, is_dsa, tabs, itabs, _lane_gain(attn_q_norm[kv_slot]),
                           _lane_gain(attn_k_norm[kv_slot]))
            q_all, k_all, kmean = outs[0], outs[1], outs[2].reshape(r // ROW_TILE, ATT_W)
            if is_dsa:
                qi_all, kit_all = outs[4], outs[5]
                mask = dsa_select_prompt(qi_all, kit_all, proj, n, t)
                o_p = prompt_attention(q_all, k_all, proj, mask, True, n, t)
                key_bias, new_bias = dsa_select_sample(cache_idx[j], page_table, qi_all, kit_all,
                                                       proj, rp, ns, ts)
                new_idx.append(kit_all[:, :IDX_DIM])
            else:
                o_p = prompt_attention(q_all, k_all, proj, kmean, False, n, t)
                key_bias, new_bias = zero_key_bias, causal_new
            o_s = sample_attention(kt_all, vt_all, kv_slot, page_table, q_all, k_all, proj,
                                   key_bias, new_bias, not is_dsa, rp, ns)
            new_k.append(k_all)
            new_v.append(outs[3])
            kv_slot += 1
        mq_gain = _lane_gain(mem_q_norm[l])
        om_p = mem_attention(proj, memq_col, mq_gain, mem_k_p[l], mem_v_p[l], 0, n, t, tq_mem)
        om_s = mem_attention(proj, memq_col, mq_gain, mem_k_s[l], mem_v_s[l], rp, ns, SAMPLE_ROWS,
                             SAMPLE_ROWS)
        tail = r - rp - rs
        o_all = jnp.concatenate([o_p, o_s, jnp.zeros((tail, ATT_W), F32)], axis=0)
        om_all = jnp.concatenate([om_p, om_s, jnp.zeros((tail, MEM_W), F32)], axis=0)
        x_all = post_block(x_all, o_all, om_all, w_out_bf, norm_ffn[l], w_up_bf, w_down_bf, l,
                           tm_ffn, 2048)

    def split(arrs, width_shape):
        ap = jnp.stack([a[:rp] for a in arrs]).reshape(len(arrs), n, t, *width_shape)
        asmp = jnp.stack([a[rp:rp + rs] for a in arrs]).reshape(
            len(arrs), ns, SAMPLE_ROWS, *width_shape)[:, :, :ts]
        return ap, asmp

    y_prompt = x_all[:rp].reshape(n, t, d)
    y_sample = x_all[rp:rp + rs].reshape(ns, SAMPLE_ROWS, d)[:, :ts]
    k_p, k_s = split(new_k, (N_HEADS, HEAD_DIM))
    v_p, v_s = split(new_v, (N_HEADS, HEAD_DIM))
    idx_p, idx_s = split(new_idx, (IDX_DIM,))
    mem_k_out = mem_k_p.reshape(depth, n, mem_len, MEM_HEADS, HEAD_DIM)
    mem_v_out = mem_v_p.reshape(depth, n, mem_len, MEM_HEADS, HEAD_DIM)
    return (y_prompt, y_sample, k_p, v_p, idx_p, mem_k_out, mem_v_out,
            jnp.stack(new_delta_p), jnp.stack(new_conv_p), k_s, v_s, idx_s,
            jnp.stack(new_delta_s), jnp.stack(new_conv_s))
```

```python
import functools
import math

import jax
import jax.numpy as jnp
from jax import lax
from jax.experimental import pallas as pl
from jax.experimental.pallas import tpu as pltpu

F32 = jnp.float32
BF16 = jnp.bfloat16
I32 = jnp.int32
HI = lax.Precision.HIGHEST

D_MODEL = 1024
HEAD_DIM = 64
N_HEADS = 12
ATT_W = N_HEADS * HEAD_DIM
MEM_HEADS = 4
MEM_W = MEM_HEADS * HEAD_DIM
ROT_DIM = HEAD_DIM // 4
ROPE_THETA = 500000.0
ATT_SCALE = HEAD_DIM ** -0.5
LOG2E = math.log2(math.e)
MOBA_BLOCK = 256
MOBA_TOPK = 3
IDX_HEADS = 8
IDX_DIM = 32
IDX_ROT = IDX_DIM // 4
DSA_TOPK = 256
GDN_HEADS = 6
GDN_DK = 128
GDN_W = GDN_HEADS * GDN_DK
CONV_W = 4
CONV_CH = 3 * GDN_W
GDN_CHUNK = 64
D_FF = 4 * D_MODEL
EPS = 1e-6
NEG = -1e30
PAGE_SIZE = 128
N_MIXERS = 3

LANES = 128
SUBLANES = 8
BF16_ROWS = 16
ROW_TILE = 256
SAMPLE_ROWS = 8
VMEM_LIMIT = 56 * 1024 * 1024
INT_MIN = -2 ** 31
KEYS_PER_ROW = LANES // IDX_DIM
SUB_SPACING_STEPS = (0.5, 0.25, 0.125, 0.0625)
PAGES_PER_STEP = 8

MOBA_C = 3 * ATT_W + MEM_W
DSA_C = 3 * ATT_W + 2 * MEM_W + LANES
GDN_C = 4 * GDN_W + MEM_W + LANES


def _cparams(sem):
    return pltpu.CompilerParams(dimension_semantics=sem, vmem_limit_bytes=VMEM_LIMIT)


def _dot_nt(a, b, precision=None):
    return lax.dot_general(a, b, (((1,), (1,)), ((), ())), precision=precision,
                           preferred_element_type=F32)


def _dot3(a, b, dims):
    a_hi = a.astype(BF16)
    b_hi = b.astype(BF16)
    a_lo = (a - a_hi.astype(F32)).astype(BF16)
    b_lo = (b - b_hi.astype(F32)).astype(BF16)
    dg = lambda x, y: lax.dot_general(x, y, dims, preferred_element_type=F32)
    return dg(a_hi, b_hi) + dg(a_hi, b_lo) + dg(a_lo, b_hi)


def _head_block_ones(width):
    r = lax.broadcasted_iota(I32, (LANES, LANES), 0) // width
    c = lax.broadcasted_iota(I32, (LANES, LANES), 1) // width
    return jnp.where(r == c, 1.0, 0.0).astype(BF16)


def _group_sum(x2, gmat):
    hi = x2.astype(BF16)
    lo = (x2 - hi.astype(F32)).astype(BF16)
    return (jnp.dot(hi, gmat, preferred_element_type=F32)
            + jnp.dot(lo, gmat, preferred_element_type=F32))


def _head_rms(x, gain, gmat, width):
    ms = _group_sum(x * x, gmat) * (1.0 / width)
    return x * lax.rsqrt(ms + EPS) * gain


def _rope(y, c, sa, sb, half):
    return y * c + pltpu.roll(y, LANES - half, 1) * sa + pltpu.roll(y, half, 1) * sb


def _norm_matmul_kernel(x_ref, g_ref, w_ref, o_ref):
    x = x_ref[...]
    ms = jnp.mean(x * x, axis=-1, keepdims=True)
    h = (x * lax.rsqrt(ms + EPS) * g_ref[...]).astype(BF16)
    o_ref[...] = jnp.dot(h, w_ref[...], preferred_element_type=F32)


def norm_matmul(x, g, w_all, j):
    r, d = x.shape
    c = w_all.shape[2]
    tm = ROW_TILE
    return pl.pallas_call(
        _norm_matmul_kernel,
        grid=(r // tm,),
        in_specs=[pl.BlockSpec((tm, d), lambda i: (i, 0)),
                  pl.BlockSpec((1, d), lambda i: (0, 0)),
                  pl.BlockSpec((None, d, c), lambda i: (j, 0, 0))],
        out_specs=pl.BlockSpec((tm, c), lambda i: (i, 0)),
        out_shape=jax.ShapeDtypeStruct((r, c), F32),
        compiler_params=_cparams(("parallel",)),
        name="norm_matmul",
    )(x, g.reshape(1, d), w_all)


def _matmul_kernel(x_ref, w_ref, o_ref):
    o_ref[...] = jnp.dot(x_ref[...].astype(BF16), w_ref[...], preferred_element_type=F32)


def layer_matmul(x, w):
    r, d = x.shape
    nl, _, c = w.shape
    return pl.pallas_call(
        _matmul_kernel,
        grid=(nl,),
        in_specs=[pl.BlockSpec((r, d), lambda l: (0, 0)),
                  pl.BlockSpec((None, d, c), lambda l: (l, 0, 0))],
        out_specs=pl.BlockSpec((None, r, c), lambda l: (l, 0, 0)),
        out_shape=jax.ShapeDtypeStruct((nl, r, c), F32),
        compiler_params=_cparams(("parallel",)),
        name="mem_kv_matmul",
    )(x, w)


def _mem_k_norm_kernel(kv_ref, g_ref, o_ref):
    gmat = _head_block_ones(HEAD_DIM)
    g = g_ref[...]
    for p in range(MEM_W // LANES):
        x = kv_ref[:, p * LANES:(p + 1) * LANES]
        o_ref[:, p * LANES:(p + 1) * LANES] = _head_rms(x, g, gmat, HEAD_DIM)


def mem_key_norm(kv, gains):
    nl, r, _ = kv.shape
    return pl.pallas_call(
        _mem_k_norm_kernel,
        grid=(nl,),
        in_specs=[pl.BlockSpec((None, r, MEM_W), lambda l: (l, 0, 0)),
                  pl.BlockSpec((None, 1, LANES), lambda l: (l, 0, 0))],
        out_specs=pl.BlockSpec((None, r, MEM_W), lambda l: (l, 0, 0)),
        out_shape=jax.ShapeDtypeStruct((nl, r, MEM_W), F32),
        compiler_params=_cparams(("parallel",)),
        name="mem_k_norm",
    )(kv, gains)


def _qk_post_kernel(is_dsa, np_tiles, *refs):
    if is_dsa:
        (q_ref, k_ref, v_ref, qi_ref, kw_ref, c_ref, sa_ref, sb_ref, ci_ref, sai_ref, sbi_ref,
         qg_ref, kg_ref, qo_ref, kop_ref, kos_ref, km_ref, vop_ref, vos_ref, qio_ref, kit_ref) = refs
    else:
        (q_ref, k_ref, v_ref, c_ref, sa_ref, sb_ref, qg_ref, kg_ref,
         qo_ref, kop_ref, kos_ref, km_ref, vop_ref, vos_ref) = refs
    is_prompt = pl.program_id(0) < np_tiles

    def put(p_ref, s_ref, sl, val):
        @pl.when(is_prompt)
        def _():
            p_ref[:, sl] = val

        @pl.when(jnp.logical_not(is_prompt))
        def _():
            s_ref[:, sl] = val

    put(vop_ref, vos_ref, slice(None), v_ref[...])
    gmat = _head_block_ones(HEAD_DIM)
    c, sa, sb = c_ref[...], sa_ref[...], sb_ref[...]
    qg, kg = qg_ref[...], kg_ref[...]
    half = ROT_DIM // 2
    for p in range(ATT_W // LANES):
        sl = slice(p * LANES, (p + 1) * LANES)
        qn = _rope(_head_rms(q_ref[:, sl], qg, gmat, HEAD_DIM), c, sa, sb, half)
        qo_ref[:, sl] = qn * (ATT_SCALE * LOG2E)
        kn = _rope(_head_rms(k_ref[:, sl], kg, gmat, HEAD_DIM), c, sa, sb, half)
        put(kop_ref, kos_ref, sl, kn)
        km_ref[:, sl] = jnp.sum(kn, axis=0, keepdims=True) * (1.0 / ROW_TILE)
    if is_dsa:
        ci, sai, sbi = ci_ref[...], sai_ref[...], sbi_ref[...]
        ihalf = IDX_ROT // 2
        for p in range(IDX_HEADS * IDX_DIM // LANES):
            sl = slice(p * LANES, (p + 1) * LANES)
            qio_ref[:, sl] = _rope(qi_ref[:, sl], ci, sai, sbi, ihalf)
        kw = kw_ref[...]
        lane = lax.broadcasted_iota(I32, kw.shape, 1)
        kz = jnp.where(lane < IDX_DIM, kw, 0.0)
        kt = kz
        for s in range(1, LANES // IDX_DIM):
            kt = kt + pltpu.roll(kz, s * IDX_DIM, 1)
        kit_ref[...] = _rope(kt, ci, sai, sbi, ihalf)


def qk_post(proj, is_dsa, tabs, itabs, q_gain, k_gain, rp):
    r = proj.shape[0]
    tm = ROW_TILE
    np_tiles = rp // tm
    assert rp % tm == 0 and r > rp
    row = lambda cb: (lambda i: (i, cb))
    prow = lambda i: (jnp.minimum(i, np_tiles - 1), 0)
    srow = lambda i: (jnp.maximum(i - np_tiles, 0), 0)
    group_specs = [pl.BlockSpec((tm, ATT_W), prow), pl.BlockSpec((tm, ATT_W), srow)]
    group_shapes = [jax.ShapeDtypeStruct((rp, ATT_W), F32), jax.ShapeDtypeStruct((r - rp, ATT_W), F32)]
    tab_spec = pl.BlockSpec((tm, LANES), lambda i: (i, 0))
    gain_spec = pl.BlockSpec((1, LANES), lambda i: (0, 0))
    in_specs = [pl.BlockSpec((tm, ATT_W), row(0)), pl.BlockSpec((tm, ATT_W), row(1)),
                pl.BlockSpec((tm, ATT_W), row(2))]
    args = [proj, proj, proj]
    if is_dsa:
        in_specs += [pl.BlockSpec((tm, MEM_W), row(3 * ATT_W // MEM_W)),
                     pl.BlockSpec((tm, LANES), row((3 * ATT_W + 2 * MEM_W) // LANES))]
        args += [proj, proj]
    in_specs += [tab_spec] * 3
    args += list(tabs)
    if is_dsa:
        in_specs += [tab_spec] * 3
        args += list(itabs)
    in_specs += [gain_spec, gain_spec]
    args += [q_gain, k_gain]
    out_specs = ([pl.BlockSpec((tm, ATT_W), row(0))] + group_specs
                 + [pl.BlockSpec((None, 1, ATT_W), lambda i: (i, 0, 0))] + group_specs)
    out_shape = ([jax.ShapeDtypeStruct((r, ATT_W), F32)] + group_shapes
                 + [jax.ShapeDtypeStruct((r // tm, 1, ATT_W), F32)] + group_shapes)
    if is_dsa:
        out_specs += [pl.BlockSpec((tm, MEM_W), row(0)), pl.BlockSpec((tm, LANES), row(0))]
        out_shape += [jax.ShapeDtypeStruct((r, MEM_W), F32), jax.ShapeDtypeStruct((r, LANES), F32)]
    return pl.pallas_call(
        functools.partial(_qk_post_kernel, is_dsa, np_tiles),
        grid=(r // tm,),
        in_specs=in_specs, out_specs=out_specs, out_shape=out_shape,
        compiler_params=_cparams(("arbitrary",)),
        name="qk_post_dsa" if is_dsa else "qk_post",
    )(*args)


def _post_kernel(x_ref, o_ref, om_ref, wa_ref, wb_ref, g_ref, wu_ref, wd_ref, y_ref, h_sc):
    f = pl.program_id(1)

    @pl.when(f == 0)
    def _():
        x1 = (x_ref[...]
              + jnp.dot(o_ref[...].astype(BF16), wa_ref[...], preferred_element_type=F32)
              + jnp.dot(om_ref[...].astype(BF16), wb_ref[...], preferred_element_type=F32))
        ms = jnp.mean(x1 * x1, axis=-1, keepdims=True)
        h_sc[...] = (x1 * lax.rsqrt(ms + EPS) * g_ref[...]).astype(BF16)
        y_ref[...] = x1

    up = jnp.dot(h_sc[...], wu_ref[...], preferred_element_type=F32)
    act = jnp.square(jnp.maximum(up, 0.0)).astype(BF16)
    y_ref[...] += jnp.dot(act, wd_ref[...], preferred_element_type=F32)


def post_block(x, o, om, w_out, g, w_up, w_down, l, tm, tf):
    r, d = x.shape
    return pl.pallas_call(
        _post_kernel,
        grid=(r // tm, D_FF // tf),
        in_specs=[pl.BlockSpec((tm, d), lambda i, f: (i, 0)),
                  pl.BlockSpec((tm, ATT_W), lambda i, f: (i, 0)),
                  pl.BlockSpec((tm, MEM_W), lambda i, f: (i, 0)),
                  pl.BlockSpec((None, ATT_W, d), lambda i, f: (l, 0, 0)),
                  pl.BlockSpec((None, MEM_W, d), lambda i, f: (l, ATT_W // MEM_W, 0)),
                  pl.BlockSpec((1, d), lambda i, f: (0, 0)),
                  pl.BlockSpec((None, d, tf), lambda i, f: (l, 0, f)),
                  pl.BlockSpec((None, tf, d), lambda i, f: (l, f, 0))],
        out_specs=pl.BlockSpec((tm, d), lambda i, f: (i, 0)),
        out_shape=jax.ShapeDtypeStruct((r, d), F32),
        scratch_shapes=[pltpu.VMEM((tm, d), BF16)],
        compiler_params=_cparams(("parallel", "arbitrary")),
        name="out_proj_ffn",
    )(x, o, om, w_out, w_out, g.reshape(1, d), w_up, w_down)


def _topk_block_bias(gate, n_past, n_top):
    nbk = gate.shape[1]
    col = lax.broadcasted_iota(I32, gate.shape, 1)
    past = col < n_past
    g = jnp.where(past, gate, NEG)
    sel = jnp.zeros(gate.shape, F32)
    for _ in range(n_top):
        mx = jnp.max(g, axis=1, keepdims=True)
        idx = jnp.min(jnp.where(g == mx, col, nbk), axis=1, keepdims=True)
        pick = col == idx
        sel = jnp.where(pick, 1.0, sel)
        g = jnp.where(pick, -jnp.inf, g)
    return jnp.where(past, jnp.where(sel > 0.5, 0.0, NEG), NEG)


def _topk_block_bias_t(gate, n_past, n_top):
    nbk = gate.shape[0]
    row = lax.broadcasted_iota(I32, gate.shape, 0)
    past = row < n_past
    g = jnp.where(past, gate, NEG)
    sel = jnp.zeros(gate.shape, F32)
    for _ in range(n_top):
        mx = jnp.max(g, axis=0, keepdims=True)
        idx = jnp.min(jnp.where(g == mx, row, nbk), axis=0, keepdims=True)
        pick = row == idx
        sel = jnp.where(pick, 1.0, sel)
        g = jnp.where(pick, -jnp.inf, g)
    return jnp.where(past, jnp.where(sel > 0.5, 0.0, NEG), NEG)


def _prompt_attn_kernel(is_dsa, nb, n_top, q_ref, k_ref, v_ref, aux_ref, o_ref,
                        kbf_sc, vt_sc, m_sc, acc_sc, bias_sc):
    blk = MOBA_BLOCK
    i = pl.program_id(2)

    @pl.when(i == 0)
    def _():
        kbf_sc[...] = k_ref[...].astype(BF16)
        ones_row = jnp.where(lax.broadcasted_iota(I32, (BF16_ROWS, blk), 0) == 0, 1.0, 0.0)
        for c in range(nb):
            vt_sc[c, 0:LANES, :] = jnp.transpose(v_ref[c * blk:(c + 1) * blk, :]).astype(BF16)
            vt_sc[c, LANES:LANES + BF16_ROWS, :] = ones_row.astype(BF16)

    q = q_ref[...]
    lane = lax.broadcasted_iota(I32, q.shape, 1)
    qcat = jnp.concatenate([jnp.where(lane // HEAD_DIM == hh, q, 0.0) for hh in range(2)], axis=0)
    qcat_bf = qcat.astype(BF16)
    m_sc[...] = jnp.full(m_sc.shape, NEG, F32)
    acc_sc[...] = jnp.zeros(acc_sc.shape, F32)

    def scores(j):
        st = _dot_nt(kbf_sc[pl.ds(pl.multiple_of(j * blk, blk), blk), :], qcat_bf)
        if is_dsa:
            mb = aux_ref[j].astype(F32)
            st = st + jnp.concatenate([mb, mb], axis=1)
        return st

    def update(js, sts, col_bias):
        m_old = m_sc[...]
        m_new = m_old
        for st, cb in zip(sts, col_bias):
            mx = jnp.max(st, axis=0, keepdims=True)
            m_new = jnp.maximum(m_new, mx if cb is None else mx + cb)
        acc = jnp.exp2(m_old - m_new) * acc_sc[...]
        for j, st, cb in zip(js, sts, col_bias):
            p = jnp.exp2((st - (m_new if cb is None else m_new - cb)).astype(BF16))
            acc = acc + jnp.dot(vt_sc[j], p, preferred_element_type=F32)
        acc_sc[...] = acc
        m_sc[...] = m_new

    if is_dsa:
        col_bias_of = lambda j: None
        count = i + 1
    else:
        gate = _dot_nt(aux_ref[...], qcat, HI)
        bias_sc[...] = _topk_block_bias_t(gate, i, n_top)
        kr = lax.broadcasted_iota(I32, (blk, 2 * blk), 0)
        qc = lax.broadcasted_iota(I32, (blk, 2 * blk), 1) % blk
        update([i], [scores(i) + jnp.where(kr <= qc, 0.0, NEG)], [None])
        col_bias_of = lambda j: bias_sc[pl.ds(j, 1), :]
        count = i

    npairs = count // 2

    def pair(jj, sts):
        nxt = jnp.minimum(jj + 1, npairs - 1)
        nxt_sts = (scores(2 * nxt), scores(2 * nxt + 1))
        js = [2 * jj, 2 * jj + 1]
        update(js, sts, [col_bias_of(j) for j in js])
        return nxt_sts

    lax.fori_loop(0, npairs, pair, (scores(0), scores(1)))

    @pl.when(count % 2 == 1)
    def _():
        update([count - 1], [scores(count - 1)], [col_bias_of(count - 1)])

    acc = acc_sc[...]
    res = acc[0:LANES, :] / acc[LANES:LANES + 1, :]
    drow = lax.broadcasted_iota(I32, (LANES, blk), 0)
    o_t = jnp.where(drow // HEAD_DIM == 0, res[:, :blk], res[:, blk:])
    o_ref[...] = jnp.transpose(o_t)


def prompt_attention(q_all, k_all, proj, aux, is_dsa, n, t):
    blk = MOBA_BLOCK
    nb = t // blk
    npair = ATT_W // LANES
    n_top = min(MOBA_TOPK, nb)
    vcol = 2 * ATT_W // LANES
    if is_dsa:
        aux_spec = pl.BlockSpec((None, None, nb, blk, blk), lambda b, p, i: (b, i, 0, 0, 0))
    else:
        aux_spec = pl.BlockSpec((nb, LANES), lambda b, p, i: (b, p))
    return pl.pallas_call(
        functools.partial(_prompt_attn_kernel, is_dsa, nb, n_top),
        grid=(n, npair, nb),
        in_specs=[pl.BlockSpec((blk, LANES), lambda b, p, i: (b * nb + i, p)),
                  pl.BlockSpec((t, LANES), lambda b, p, i: (b, p)),
                  pl.BlockSpec((t, LANES), lambda b, p, i: (b, vcol + p)),
                  aux_spec],
        out_specs=pl.BlockSpec((blk, LANES), lambda b, p, i: (b * nb + i, p)),
        out_shape=jax.ShapeDtypeStruct((n * t, ATT_W), F32),
        scratch_shapes=[pltpu.VMEM((t, LANES), BF16), pltpu.VMEM((nb, LANES + BF16_ROWS, blk), BF16),
                        pltpu.VMEM((1, 2 * blk), F32),
                        pltpu.VMEM((LANES + BF16_ROWS, 2 * blk), F32), pltpu.VMEM((nb, 2 * blk), F32)],
        compiler_params=_cparams(("parallel", "parallel", "arbitrary")),
        name="prompt_attn_dsa" if is_dsa else "prompt_attn_moba",
    )(q_all, k_all, proj, aux)


def _key_to_float(k):
    b = jnp.where(k < 0, k ^ jnp.int32(0x7FFFFFFF), k)
    return lax.bitcast_convert_type(b, F32)


def _select_topk(count_ge, count_gt, count_tie_lt, k_sel, shape, idx_bits):
    def vbit(b, t):
        cand = t + lax.shift_left(jnp.int32(1), 31 - b)
        return jnp.where(count_ge(_key_to_float(cand)) >= k_sel, cand, t)

    vkey = lax.fori_loop(0, 32, vbit, jnp.full(shape, INT_MIN, I32))
    v = _key_to_float(vkey)
    ulp = _key_to_float(vkey + 1) - v
    for frac in SUB_SPACING_STEPS:
        cand = v + ulp * frac
        v = jnp.where(count_ge(cand) >= k_sel, cand, v)
    need = k_sel - count_gt(v)

    def cbit(b, c):
        cand = c + lax.shift_left(jnp.int32(1), idx_bits - 1 - b)
        return jnp.where(count_tie_lt(v, cand) <= need - 1, cand, c)

    all_ties_taken = jnp.all(count_ge(v) == k_sel)
    c = lax.cond(all_ties_taken,
                 lambda: jnp.full(shape, (1 << idx_bits) - 1, I32),
                 lambda: lax.fori_loop(0, idx_bits, cbit, jnp.zeros(shape, I32)))
    return v, c


def _dsa_select_kernel(nb, k_sel, idx_bits, qi_ref, kit_ref, w_ref, o_ref, sc_sc):
    blk = MOBA_BLOCK
    i = pl.program_id(1)
    w_t = jnp.transpose(w_ref[...])
    lane = lax.broadcasted_iota(I32, (blk, LANES), 1)
    qm, wrow = [], []
    for h in range(IDX_HEADS):
        seg = qi_ref[:, (h // 4) * LANES:(h // 4 + 1) * LANES]
        qm.append(jnp.where(lane // IDX_DIM == h % 4, seg, 0.0).astype(BF16))
        wrow.append(w_t[IDX_DIM + h:IDX_DIM + h + 1, :] * (1.0 / 16.0))
    kr = lax.broadcasted_iota(I32, (blk, blk), 0)
    qpos = i * blk + lax.broadcasted_iota(I32, (blk, blk), 1)

    def score_block(c, carry):
        off = pl.multiple_of(c * blk, blk)
        kt = kit_ref[pl.ds(off, blk), :].astype(BF16)
        acc = jnp.zeros((blk, blk), F32)
        for h in range(IDX_HEADS):
            acc = acc + jnp.maximum(_dot_nt(kt, qm[h]), 0.0) * wrow[h]
        sc_sc[c] = jnp.where(c * blk + kr <= qpos, acc + 0.0, -jnp.inf)
        return carry

    lax.fori_loop(0, i + 1, score_block, 0)

    def fold(x):
        return jnp.sum(x.reshape(blk // SUBLANES, SUBLANES, blk), axis=0)

    def count_ge(cand):
        def body(c, acc):
            return acc + fold(jnp.where(sc_sc[c] >= cand, 1, 0))

        acc = lax.fori_loop(0, i + 1, body, jnp.zeros((SUBLANES, blk), I32))
        return jnp.sum(acc, axis=0, keepdims=True)

    def count_gt(cand):
        def body(c, acc):
            return acc + fold(jnp.where(sc_sc[c] > cand, 1, 0))

        acc = lax.fori_loop(0, i + 1, body, jnp.zeros((SUBLANES, blk), I32))
        return jnp.sum(acc, axis=0, keepdims=True)

    def count_tie_lt(v, cut):
        def body(c, acc):
            hit = (sc_sc[c] == v) & (c * blk + kr < cut)
            return acc + fold(jnp.where(hit, 1, 0))

        acc = lax.fori_loop(0, i + 1, body, jnp.zeros((SUBLANES, blk), I32))
        return jnp.sum(acc, axis=0, keepdims=True)

    v, cut = _select_topk(count_ge, count_gt, count_tie_lt, k_sel, (1, blk), idx_bits)
    take_all = i * blk + lax.broadcasted_iota(I32, (1, blk), 1) + 1 <= k_sel

    def emit(c, carry):
        s = sc_sc[c]
        kpos = c * blk + kr
        bias = jnp.where(s > v, 0.0, jnp.where((s == v) & (kpos <= cut), 0.0, NEG))
        bias = jnp.where(take_all, 0.0, bias)
        o_ref[c] = jnp.where(kpos <= qpos, bias, NEG).astype(BF16)
        return carry

    lax.fori_loop(0, i + 1, emit, 0)

    def fill(c, carry):
        o_ref[c] = jnp.full((blk, blk), NEG, BF16)
        return carry

    lax.fori_loop(i + 1, nb, fill, 0)


def dsa_select_prompt(qi, kit, proj, n, t):
    blk = MOBA_BLOCK
    nb = t // blk
    k_sel = min(DSA_TOPK, t // 4)
    idx_bits = max(1, (t - 1).bit_length())
    wcol = (3 * ATT_W + 2 * MEM_W) // LANES
    return pl.pallas_call(
        functools.partial(_dsa_select_kernel, nb, k_sel, idx_bits),
        grid=(n, nb),
        in_specs=[pl.BlockSpec((blk, MEM_W), lambda b, i: (b * nb + i, 0)),
                  pl.BlockSpec((t, LANES), lambda b, i: (b, 0)),
                  pl.BlockSpec((blk, LANES), lambda b, i: (b * nb + i, wcol))],
        out_specs=pl.BlockSpec((None, None, nb, blk, blk), lambda b, i: (b, i, 0, 0, 0)),
        out_shape=jax.ShapeDtypeStruct((n, nb, nb, blk, blk), BF16),
        scratch_shapes=[pltpu.VMEM((nb, blk, blk), F32)],
        compiler_params=_cparams(("parallel", "arbitrary")),
        name="dsa_select_prompt",
    )(qi, kit, proj)


def _mem_attn_kernel(q_ref, g_ref, mk_ref, mv_ref, o_ref):
    gmat = _head_block_ones(HEAD_DIM)
    g = g_ref[...]
    for p in range(MEM_W // LANES):
        sl = slice(p * LANES, (p + 1) * LANES)
        q = _head_rms(q_ref[:, sl], g, gmat, HEAD_DIM) * ATT_SCALE
        lane = lax.broadcasted_iota(I32, q.shape, 1)
        kb = mk_ref[:, sl].astype(BF16)
        vb = mv_ref[:, sl].astype(BF16)
        outs = []
        for hh in range(2):
            qh = jnp.where(lane // HEAD_DIM == hh, q, 0.0).astype(BF16)
            s = _dot_nt(qh, kb)
            pexp = jnp.exp(s - jnp.max(s, axis=1, keepdims=True))
            den = jnp.sum(pexp, axis=1, keepdims=True)
            outs.append(jnp.dot(pexp.astype(BF16), vb, preferred_element_type=F32) / den)
        o_ref[:, sl] = jnp.where(lane // HEAD_DIM == 0, outs[0], outs[1])


def mem_attention(proj, qcol, gain, mk, mv, row0, n, t, tq):
    nq = t // tq
    rb0 = row0 // tq
    ml = mk.shape[1]
    return pl.pallas_call(
        _mem_attn_kernel,
        grid=(n, nq),
        in_specs=[pl.BlockSpec((tq, MEM_W), lambda b, i: (rb0 + b * nq + i, qcol)),
                  pl.BlockSpec((1, LANES), lambda b, i: (0, 0)),
                  pl.BlockSpec((None, ml, MEM_W), lambda b, i: (b, 0, 0)),
                  pl.BlockSpec((None, ml, MEM_W), lambda b, i: (b, 0, 0))],
        out_specs=pl.BlockSpec((tq, MEM_W), lambda b, i: (b * nq + i, 0)),
        out_shape=jax.ShapeDtypeStruct((n * t, MEM_W), F32),
        compiler_params=_cparams(("parallel", "parallel")),
        name="mem_attn",
    )(proj, gain, mk, mv)


def _tile_rows(x, reps):
    return jnp.concatenate([x] * reps, axis=0)


def _block_diag_queries(q8):
    qt = _tile_rows(q8, N_HEADS)
    row = lax.broadcasted_iota(I32, qt.shape, 0)
    lane = lax.broadcasted_iota(I32, qt.shape, 1)
    return jnp.where(lane // HEAD_DIM == row // SAMPLE_ROWS, qt, 0.0)


def _sample_attn_kernel(nsteps, nbp, n_top, use_gate, *refs):
    npp = PAGES_PER_STEP
    pt_ref = refs[0]
    k_refs, v_refs = refs[1:1 + npp], refs[1 + npp:1 + 2 * npp]
    (q_ref, kn_ref, vn_ref, km_ref, nbias_ref, o_ref,
     qbd_sc, kmean_sc, m_sc, l_sc, acc_sc) = refs[1 + 2 * npp:]
    j = pl.program_id(1)
    ppb = MOBA_BLOCK // PAGE_SIZE

    @pl.when(j == 0)
    def _():
        qbd_sc[...] = _block_diag_queries(q_ref[...])
        kmean_sc[...] = jnp.zeros(kmean_sc.shape, F32)

    qbd = qbd_sc[...]
    qbd_bf = qbd.astype(BF16)
    kmask = km_ref[...]
    ones = jnp.ones((PAGE_SIZE, LANES), BF16)
    lane = lax.broadcasted_iota(I32, (ATT_W, LANES), 1)
    for blk in range(npp // ppb):
        bi = (npp // ppb) * j + blk
        scores, ksum = [], jnp.zeros((ATT_W, LANES), F32)
        for pg in range(ppb):
            kf = k_refs[blk * ppb + pg][...]
            khi = kf.astype(BF16)
            scores.append(jnp.dot(qbd_bf, khi, preferred_element_type=F32))
            if use_gate:
                klo = (kf - khi.astype(F32)).astype(BF16)
                ksum = (ksum + jnp.dot(khi, ones, preferred_element_type=F32)
                        + jnp.dot(klo, ones, preferred_element_type=F32))
        if use_gate:
            kmean_sc[...] += jnp.where(lane == bi, ksum * (1.0 / MOBA_BLOCK), 0.0)
        s = jnp.concatenate(scores, axis=1)
        s = s + _tile_rows(kmask[:, blk * MOBA_BLOCK:(blk + 1) * MOBA_BLOCK], N_HEADS)
        m = jnp.max(s, axis=1, keepdims=True)
        p = jnp.exp2(s - m)
        pb = p.astype(BF16)
        acc = jnp.zeros((N_HEADS * SAMPLE_ROWS, ATT_W), F32)
        for pg in range(ppb):
            vb = v_refs[blk * ppb + pg][...].astype(BF16)
            acc = acc + _dot_nt(pb[:, pg * PAGE_SIZE:(pg + 1) * PAGE_SIZE], vb)
        m_sc[bi] = m
        l_sc[bi] = jnp.sum(p, axis=1, keepdims=True)
        acc_sc[bi] = acc

    @pl.when(j == nsteps - 1)
    def _():
        pad = jnp.zeros((PAGE_SIZE - SAMPLE_ROWS, ATT_W), F32)
        kn = jnp.concatenate([kn_ref[...], pad], axis=0).astype(BF16)
        vn = jnp.concatenate([vn_ref[...], pad], axis=0).astype(BF16)
        s = _dot_nt(qbd_bf, kn) + _tile_rows(nbias_ref[...], N_HEADS)
        m_n = jnp.max(s, axis=1, keepdims=True)
        p = jnp.exp2(s - m_n)
        l_n = jnp.sum(p, axis=1, keepdims=True)
        acc_n = jnp.dot(p.astype(BF16), vn, preferred_element_type=F32)
        if use_gate:
            gate = jnp.dot(qbd, kmean_sc[...], precision=HI, preferred_element_type=F32)
            bias = _topk_block_bias(gate, nbp, n_top)
            mb = [m_sc[b] + bias[:, b:b + 1] for b in range(nbp)]
        else:
            mb = [m_sc[b] for b in range(nbp)]
        m_all = m_n
        for b in range(nbp):
            m_all = jnp.maximum(m_all, mb[b])
        w_n = jnp.exp2(m_n - m_all)
        den = w_n * l_n
        num = w_n * acc_n
        for b in range(nbp):
            w_b = jnp.exp2(mb[b] - m_all)
            den = den + w_b * l_sc[b]
            num = num + w_b * acc_sc[b]
        res = num / den
        olane = lax.broadcasted_iota(I32, (SAMPLE_ROWS, ATT_W), 1)
        out = jnp.zeros((SAMPLE_ROWS, ATT_W), F32)
        for h in range(N_HEADS):
            out = out + jnp.where(olane // HEAD_DIM == h,
                                  res[h * SAMPLE_ROWS:(h + 1) * SAMPLE_ROWS, :], 0.0)
        o_ref[...] = out


def sample_attention(kt_all, vt_all, slot, page_table, q_all, k_new, v_new, key_bias, new_bias,
                     use_gate, row0, ns):
    npg = page_table.shape[1]
    npp = PAGES_PER_STEP
    assert npg % npp == 0
    nsteps = npg // npp
    nbp = npg // (MOBA_BLOCK // PAGE_SIZE)
    assert nbp <= LANES
    n_top = min(MOBA_TOPK, nbp + 1)
    rb0 = row0 // SAMPLE_ROWS
    rows = N_HEADS * SAMPLE_ROWS
    page = lambda off: pl.BlockSpec((None, None, ATT_W, PAGE_SIZE),
                                    lambda b, j, pt: (slot, pt[b, npp * j + off], 0, 0))
    rowblk = lambda cb: pl.BlockSpec((SAMPLE_ROWS, ATT_W), lambda b, j, pt: (rb0 + b, cb))
    newblk = pl.BlockSpec((SAMPLE_ROWS, ATT_W), lambda b, j, pt: (b, 0))
    grid_spec = pltpu.PrefetchScalarGridSpec(
        num_scalar_prefetch=1,
        grid=(ns, nsteps),
        in_specs=[page(off) for off in range(npp)] * 2 + [
            rowblk(0), newblk, newblk,
            pl.BlockSpec((None, SAMPLE_ROWS, npp * PAGE_SIZE), lambda b, j, pt: (b, 0, j)),
            pl.BlockSpec((None, SAMPLE_ROWS, LANES), lambda b, j, pt: (b, 0, 0))],
        out_specs=pl.BlockSpec((SAMPLE_ROWS, ATT_W), lambda b, j, pt: (b, 0)),
        scratch_shapes=[pltpu.VMEM((rows, ATT_W), F32), pltpu.VMEM((ATT_W, LANES), F32),
                        pltpu.VMEM((nbp, rows, 1), F32), pltpu.VMEM((nbp, rows, 1), F32),
                        pltpu.VMEM((nbp, rows, ATT_W), F32)],
    )
    return pl.pallas_call(
        functools.partial(_sample_attn_kernel, nsteps, nbp, n_top, use_gate),
        grid_spec=grid_spec,
        out_shape=jax.ShapeDtypeStruct((ns * SAMPLE_ROWS, ATT_W), F32),
        compiler_params=_cparams(("parallel", "arbitrary")),
        name="sample_attn_moba" if use_gate else "sample_attn_dsa",
    )(page_table, *([kt_all] * npp), *([vt_all] * npp), q_all, k_new, v_new, key_bias, new_bias)


def _dsa_select_sample_kernel(npg, ts, k_sel, idx_bits, pt_ref, cidx_ref, qi_ref, kit_ref, w_ref,
                              km_ref, nb_ref, kbuf, sem):
    b = pl.program_id(0)
    past = npg * PAGE_SIZE
    kpr = PAGE_SIZE // KEYS_PER_ROW
    nrow = past // KEYS_PER_ROW

    def page_copy(p):
        return pltpu.make_async_copy(cidx_ref.at[pt_ref[b, p]], kbuf.at[pl.ds(p * kpr, kpr)], sem)

    for p in range(npg):
        page_copy(p).start()
    for p in range(npg):
        page_copy(p).wait()

    lane = lax.broadcasted_iota(I32, (SAMPLE_ROWS, LANES), 1)
    wblk = w_ref[...]
    kpast = kbuf[...].astype(BF16)
    knew = jnp.where(lane < IDX_DIM, kit_ref[...], 0.0)
    knew = jnp.concatenate([knew, jnp.zeros((LANES - SAMPLE_ROWS, LANES), F32)], axis=0).astype(BF16)
    isc_p = [jnp.zeros((SAMPLE_ROWS, nrow), F32) for _ in range(KEYS_PER_ROW)]
    isc_n = jnp.zeros((SAMPLE_ROWS, LANES), F32)
    for h in range(IDX_HEADS):
        seg = qi_ref[:, (h // 4) * LANES:(h // 4 + 1) * LANES]
        w_h = jnp.sum(jnp.where(lane == IDX_DIM + h, wblk, 0.0), axis=1, keepdims=True) * (1.0 / 16.0)
        for c in range(KEYS_PER_ROW):
            shift = ((c - h % 4) * IDX_DIM) % LANES
            qs = pltpu.roll(seg, shift, 1) if shift else seg
            qmh = jnp.where(lane // IDX_DIM == c, qs, 0.0).astype(BF16)
            isc_p[c] = isc_p[c] + jnp.maximum(_dot_nt(qmh, kpast), 0.0) * w_h
            if c == 0:
                isc_n = isc_n + jnp.maximum(_dot_nt(qmh, knew), 0.0) * w_h
    key_p = [x + 0.0 for x in isc_p]
    trow = lax.broadcasted_iota(I32, (SAMPLE_ROWS, LANES), 0)
    valid_n = (lane <= trow) & (lane < ts)
    key_n = jnp.where(valid_n, isc_n + 0.0, -jnp.inf)
    rid = lax.broadcasted_iota(I32, (SAMPLE_ROWS, nrow), 1)
    idx_p = [KEYS_PER_ROW * rid + c for c in range(KEYS_PER_ROW)]
    idx_n = past + lane
    cnt = lambda m: jnp.sum(jnp.where(m, 1, 0), axis=1, keepdims=True)

    def count_ge(cand):
        return sum([cnt(k >= cand) for k in key_p], cnt(key_n >= cand))

    def count_gt(cand):
        return sum([cnt(k > cand) for k in key_p], cnt(key_n > cand))

    def count_tie_lt(v, cut):
        return sum([cnt((k == v) & (ix < cut)) for k, ix in zip(key_p, idx_p)],
                   cnt((key_n == v) & (idx_n < cut)))

    v, cut = _select_topk(count_ge, count_gt, count_tie_lt, k_sel, (SAMPLE_ROWS, 1), idx_bits)
    trow1 = lax.broadcasted_iota(I32, (SAMPLE_ROWS, 1), 0)
    take_all = past + jnp.minimum(trow1, ts - 1) + 1 <= k_sel
    for c in range(KEYS_PER_ROW):
        sel = (key_p[c] > v) | ((key_p[c] == v) & (idx_p[c] <= cut))
        km_ref[c] = jnp.where(take_all, 0.0, jnp.where(sel, 0.0, NEG))
    sel_n = (key_n > v) | ((key_n == v) & (idx_n <= cut))
    nb_ref[...] = jnp.where(valid_n, jnp.where(take_all, 0.0, jnp.where(sel_n, 0.0, NEG)), NEG)


def dsa_select_sample(cache_idx_slot, page_table, qi, kit, proj, row0, ns, ts):
    npg = page_table.shape[1]
    past = npg * PAGE_SIZE
    k_sel = min(DSA_TOPK, (past + ts) // 4)
    idx_bits = max(1, (past + LANES - 1).bit_length())
    rb0 = row0 // SAMPLE_ROWS
    wcol = (3 * ATT_W + 2 * MEM_W) // LANES
    nrow = past // KEYS_PER_ROW
    pages = cache_idx_slot.reshape(-1, PAGE_SIZE // KEYS_PER_ROW, LANES)
    grid_spec = pltpu.PrefetchScalarGridSpec(
        num_scalar_prefetch=1,
        grid=(ns,),
        in_specs=[pl.BlockSpec(memory_space=pl.ANY),
                  pl.BlockSpec((SAMPLE_ROWS, MEM_W), lambda b, pt: (rb0 + b, 0)),
                  pl.BlockSpec((SAMPLE_ROWS, LANES), lambda b, pt: (rb0 + b, 0)),
                  pl.BlockSpec((SAMPLE_ROWS, LANES), lambda b, pt: (rb0 + b, wcol))],
        out_specs=[pl.BlockSpec((None, KEYS_PER_ROW, SAMPLE_ROWS, nrow), lambda b, pt: (b, 0, 0, 0)),
                   pl.BlockSpec((None, SAMPLE_ROWS, LANES), lambda b, pt: (b, 0, 0))],
        scratch_shapes=[pltpu.VMEM((nrow, LANES), F32), pltpu.SemaphoreType.DMA(())],
    )
    km, nbias = pl.pallas_call(
        functools.partial(_dsa_select_sample_kernel, npg, ts, k_sel, idx_bits),
        grid_spec=grid_spec,
        out_shape=[jax.ShapeDtypeStruct((ns, KEYS_PER_ROW, SAMPLE_ROWS, nrow), F32),
                   jax.ShapeDtypeStruct((ns, SAMPLE_ROWS, LANES), F32)],
        compiler_params=_cparams(("arbitrary",)),
        name="dsa_select_sample",
    )(page_table, pages, qi, kit, proj)
    return jnp.transpose(km, (0, 2, 3, 1)).reshape(ns, SAMPLE_ROWS, past), nbias


def _gdn_local_kernel(tc, cs, hps, t_valid, q_ref, k_ref, v_ref, ba_ref, cwq_ref, cwk_ref, cwv_ref,
                      cbq_ref, cbk_ref, cbv_ref, hp_ref, u_ref, w_ref, qd_ref, kd_ref, at_ref, xp_sc):
    h0 = pl.program_id(1) * hps
    c = pl.program_id(2)
    kw = CONV_W - 1

    @pl.when(c == 0)
    def _():
        for seg, cb in enumerate((cbq_ref, cbk_ref, cbv_ref)):
            xp_sc[seg, SUBLANES - kw:SUBLANES, :] = cb[...]

    acts = []
    for seg, (xr, cw) in enumerate(((q_ref, cwq_ref), (k_ref, cwk_ref), (v_ref, cwv_ref))):
        xp_sc[seg, SUBLANES:SUBLANES + tc, :] = xr[...]
        y = xp_sc[seg, SUBLANES - kw:SUBLANES - kw + tc, :] * cw[0:1, :]
        for jj in range(1, CONV_W):
            y = y + xp_sc[seg, SUBLANES - kw + jj:SUBLANES - kw + jj + tc, :] * cw[jj:jj + 1, :]
        acts.append(jax.nn.silu(y))
        xp_sc[seg, SUBLANES - kw:SUBLANES, :] = xp_sc[seg, SUBLANES + tc - kw:SUBLANES + tc, :]
    lane = lax.broadcasted_iota(I32, (tc, LANES), 1)
    lane1 = lax.broadcasted_iota(I32, (1, LANES), 1)
    ba = ba_ref[...]
    hp = hp_ref[...]
    qn, kn, va, beta, g = [], [], [], [], []
    for hh in range(hps):
        cl = slice(hh * LANES, (hh + 1) * LANES)
        h = h0 + hh
        qa, ka = acts[0][:, cl], acts[1][:, cl]
        qn.append(qa * lax.rsqrt(jnp.sum(qa * qa, axis=-1, keepdims=True) + EPS) * (GDN_DK ** -0.5))
        kn.append(ka * lax.rsqrt(jnp.sum(ka * ka, axis=-1, keepdims=True) + EPS))
        va.append(acts[2][:, cl])
        bcol = jnp.sum(jnp.where(lane == h, ba, 0.0), axis=1, keepdims=True)
        acol = jnp.sum(jnp.where(lane == GDN_HEADS + h, ba, 0.0), axis=1, keepdims=True)
        alog = jnp.sum(jnp.where(lane1 == h, hp[0:1, :], 0.0), axis=1, keepdims=True)
        dtb = jnp.sum(jnp.where(lane1 == h, hp[1:2, :], 0.0), axis=1, keepdims=True)
        beta_h = jax.nn.sigmoid(bcol)
        g_h = -jnp.exp(alog) * jax.nn.softplus(acol + dtb)
        if t_valid is not None:
            rowid = c * tc + lax.broadcasted_iota(I32, (tc, 1), 0)
            beta_h = jnp.where(rowid < t_valid, beta_h, 0.0)
            g_h = jnp.where(rowid < t_valid, g_h, 0.0)
        beta.append(beta_h)
        g.append(g_h)

    ri = lax.broadcasted_iota(I32, (cs, cs), 0)
    ci = lax.broadcasted_iota(I32, (cs, cs), 1)
    tri = ri >= ci
    stri = ri > ci
    tril = jnp.where(tri, 1.0, 0.0)
    eye = jnp.where(ri == ci, 1.0, 0.0)
    e0 = jnp.where(lax.broadcasted_iota(I32, (cs, LANES), 1) == 0, 1.0, 0.0)
    mm = lambda a, b: _dot3(a, b, (((1,), (0,)), ((), ())))
    mm_nt = lambda a, b: _dot3(a, b, (((1,), (1,)), ((), ())))
    n_sq = max(0, (cs - 1).bit_length() - 1)

    units = [(hh, slice(ch * cs, (ch + 1) * cs)) for hh in range(hps) for ch in range(tc // cs)]
    gcum = [jnp.dot(tril, jnp.broadcast_to(g[hh][rs], (cs, LANES)), precision=HI,
                    preferred_element_type=F32) for hh, rs in units]
    grow = [_dot_nt(e0, x, HI) for x in gcum]
    decay = [jnp.where(tri, jnp.exp(jnp.where(tri, gc[:, 0:cs] - gr, 0.0)), 0.0)
             for gc, gr in zip(gcum, grow)]
    kb = [kn[hh][rs] * beta[hh][rs] for hh, rs in units]
    pw = [-jnp.where(stri, mm_nt(kbc, kn[hh][rs]) * dc, 0.0)
          for kbc, (hh, rs), dc in zip(kb, units, decay)]
    tinv = [eye + p for p in pw]
    for _ in range(n_sq):
        pw = [mm(p, p) for p in pw]
        tinv = [t + mm(t, p) for t, p in zip(tinv, pw)]
    egc = [jnp.exp(x) for x in gcum]
    for i, (hh, rs) in enumerate(units):
        cl = slice(hh * LANES, (hh + 1) * LANES)
        glast = gcum[i][cs - 1:cs, :]
        u_ref[rs, cl] = mm(tinv[i], va[hh][rs] * beta[hh][rs])
        w_ref[rs, cl] = mm(tinv[i], kb[i] * egc[i])
        qd_ref[rs, cl] = qn[hh][rs] * egc[i]
        kd_ref[rs, cl] = kn[hh][rs] * jnp.exp(glast - gcum[i])
        at = jnp.where(tri, mm_nt(qn[hh][rs], kn[hh][rs]) * decay[i], 0.0)
        at_ref[rs, cl] = jnp.zeros((cs, LANES), F32)
        at_ref[rs, hh * LANES:hh * LANES + cs] = at
        at_ref[rs, hh * LANES + LANES // 2:(hh + 1) * LANES] = jnp.broadcast_to(
            jnp.exp(glast), (cs, LANES))[:, LANES // 2:]


def _gdn_scan_kernel(tc, cs, u_ref, w_ref, qd_ref, kd_ref, at_ref, z_ref, on_ref, s0_ref,
                     o_ref, so_ref, s_sc):
    c = pl.program_id(1)
    nc = pl.num_programs(1)

    @pl.when(c == 0)
    def _():
        s_sc[...] = s0_ref[...]

    mm = lambda a, b: _dot3(a, b, (((1,), (0,)), ((), ())))
    lane = lax.broadcasted_iota(I32, (cs, LANES), 1)
    on = on_ref[...]
    heads = [slice(h * LANES, (h + 1) * LANES) for h in range(GDN_HEADS)]
    state = [s_sc[h] for h in range(GDN_HEADS)]
    for ch in range(tc // cs):
        rs = slice(ch * cs, (ch + 1) * cs)
        at = [at_ref[rs, cl] for cl in heads]
        v_new = [u_ref[rs, cl] - mm(w_ref[rs, cl], s) for cl, s in zip(heads, state)]
        o = [mm(qd_ref[rs, cl], s) + mm(a[:, 0:cs], vn)
             for cl, s, a, vn in zip(heads, state, at, v_new)]
        decay_tot = [jnp.where(lane < LANES // 2, pltpu.roll(a, LANES // 2, 1), a)[0:1, :] for a in at]
        state = [s * dt + _dot3(kd_ref[rs, cl], vn, (((0,), (0,)), ((), ())))
                 for cl, s, dt, vn in zip(heads, state, decay_tot, v_new)]
        for cl, oh in zip(heads, o):
            ms = jnp.mean(oh * oh, axis=-1, keepdims=True)
            o_ref[rs, cl] = oh * lax.rsqrt(ms + EPS) * on * jax.nn.silu(z_ref[rs, cl])
    for h in range(GDN_HEADS):
        s_sc[h] = state[h]

    @pl.when(c == nc - 1)
    def _():
        so_ref[...] = s_sc[...]


def gdn_mix(proj, conv_w, conv_buf, head_params, out_norm, s0, row0, n, t, tc, cs, hps, t_valid):
    nc = t // tc
    rb0 = row0 // tc
    bacol = (4 * GDN_W + MEM_W) // LANES
    kw = CONV_W - 1
    assert cs <= LANES // 2 and GDN_HEADS % hps == 0
    hw = hps * LANES
    hb = GDN_W // hw
    seg_spec = lambda s: pl.BlockSpec((tc, hw), lambda b, h, c: (rb0 + b * nc + c, s * hb + h))
    cw_spec = lambda s: pl.BlockSpec((CONV_W, hw), lambda b, h, c: (0, s * hb + h))
    cb_spec = lambda s: pl.BlockSpec((None, kw, hw), lambda b, h, c: (b, 0, s * hb + h))
    loc_spec = pl.BlockSpec((tc, hw), lambda b, h, c: (b * nc + c, h))
    loc_shape = jax.ShapeDtypeStruct((n * t, GDN_W), F32)
    local = pl.pallas_call(
        functools.partial(_gdn_local_kernel, tc, cs, hps, t_valid),
        grid=(n, hb, nc),
        in_specs=[seg_spec(0), seg_spec(1), seg_spec(2),
                  pl.BlockSpec((tc, LANES), lambda b, h, c: (rb0 + b * nc + c, bacol)),
                  cw_spec(0), cw_spec(1), cw_spec(2), cb_spec(0), cb_spec(1), cb_spec(2),
                  pl.BlockSpec((2, LANES), lambda b, h, c: (0, 0))],
        out_specs=[loc_spec] * 5,
        out_shape=[loc_shape] * 5,
        scratch_shapes=[pltpu.VMEM((3, SUBLANES + tc, hw), F32)],
        compiler_params=_cparams(("parallel", "parallel", "arbitrary")),
        name="gdn_local",
    )(proj, proj, proj, proj, conv_w, conv_w, conv_w, conv_buf, conv_buf, conv_buf, head_params)
    wide = pl.BlockSpec((tc, GDN_W), lambda b, c: (b * nc + c, 0))
    state_spec = pl.BlockSpec((None, GDN_HEADS, GDN_DK, LANES), lambda b, c: (b, 0, 0, 0))
    return pl.pallas_call(
        functools.partial(_gdn_scan_kernel, tc, cs),
        grid=(n, nc),
        in_specs=[wide] * 5 + [
            pl.BlockSpec((tc, GDN_W), lambda b, c: (rb0 + b * nc + c, 3)),
            pl.BlockSpec((1, LANES), lambda b, c: (0, 0)),
            state_spec],
        out_specs=[wide, state_spec],
        out_shape=[loc_shape, jax.ShapeDtypeStruct((n, GDN_HEADS, GDN_DK, LANES), F32)],
        scratch_shapes=[pltpu.VMEM((GDN_HEADS, GDN_DK, LANES), F32)],
        compiler_params=_cparams(("parallel", "arbitrary")),
        name="gdn_scan",
    )(*local, proj, out_norm, s0)


def _rope_tables(pos, rot_dim, head_dim):
    half = rot_dim // 2
    inv_freq = ROPE_THETA ** (-jnp.arange(0, rot_dim, 2, dtype=F32) / rot_dim)
    ang = pos.astype(F32)[:, None] * inv_freq[None, :]
    cos, sin = jnp.cos(ang), jnp.sin(ang)
    r = pos.shape[0]
    rest = head_dim - 2 * half
    zh = jnp.zeros((r, half), F32)
    c = jnp.concatenate([cos, cos, jnp.ones((r, rest), F32)], axis=1)
    sa = jnp.concatenate([-sin, zh, jnp.zeros((r, rest), F32)], axis=1)
    sb = jnp.concatenate([zh, sin, jnp.zeros((r, rest), F32)], axis=1)
    reps = LANES // head_dim
    return tuple(jnp.tile(a, (1, reps)) for a in (c, sa, sb))


def _lane_gain(g):
    return jnp.tile(g.astype(F32), LANES // g.shape[0]).reshape(1, LANES)


def _pick_tile(r, options):
    for tm in options:
        if r % tm == 0:
            return tm
    raise ValueError(f"no row tile for {r}")


def kernel(x_prompt, x_sample, mem_prompt, cache_k, cache_v, cache_idx, cache_mem_k, cache_mem_v,
           state_delta, state_conv, page_table, norm_mix, norm_ffn, attn_q_norm, attn_k_norm,
           mem_q_norm, mem_k_norm, w_in_moba, w_in_dsa, w_in_gdn, gdn_conv, gdn_a_log, gdn_dt_bias,
           gdn_out_norm, w_mem_kv, w_out, w_up, w_down):
    n, t, d = x_prompt.shape
    ns, ts, _ = x_sample.shape
    depth = norm_mix.shape[0]
    npg = page_table.shape[1]
    past = npg * PAGE_SIZE
    mem_len = mem_prompt.shape[1]
    assert d == D_MODEL and t % MOBA_BLOCK == 0 and t >= CONV_W - 1
    assert CONV_W - 1 <= ts <= SAMPLE_ROWS and past % MOBA_BLOCK == 0
    rp = n * t
    rs = ns * SAMPLE_ROWS
    r = -(-(rp + rs) // ROW_TILE) * ROW_TILE
    page_table = page_table.astype(I32)

    xs = jnp.pad(x_sample, ((0, 0), (0, SAMPLE_ROWS - ts), (0, 0))).reshape(rs, d)
    x_all = jnp.concatenate([x_prompt.reshape(rp, d), xs, jnp.zeros((r - rp - rs, d), F32)], axis=0)
    pos_s = past + jnp.minimum(jnp.arange(SAMPLE_ROWS, dtype=I32), ts - 1)
    pos_all = jnp.concatenate([jnp.tile(jnp.arange(t, dtype=I32), n), jnp.tile(pos_s, ns),
                               jnp.zeros((r - rp - rs,), I32)])
    tabs = _rope_tables(pos_all, ROT_DIM, HEAD_DIM)
    itabs = _rope_tables(pos_all, IDX_ROT, IDX_DIM)

    zpad = lambda w, c: jnp.pad(w, ((0, 0), (0, 0), (0, c - w.shape[-1])))
    a3 = 3 * ATT_W
    qi_w = IDX_HEADS * IDX_DIM
    kw_w = IDX_DIM + IDX_HEADS
    w_moba = w_in_moba.astype(BF16)
    w_dsa = zpad(jnp.concatenate([w_in_dsa[..., :a3 + qi_w], w_in_dsa[..., a3 + qi_w + kw_w:],
                                  w_in_dsa[..., a3 + qi_w:a3 + qi_w + kw_w]], axis=-1), DSA_C).astype(BF16)
    g4 = 4 * GDN_W
    w_gdn = zpad(jnp.concatenate([w_in_gdn[..., :g4], w_in_gdn[..., g4 + 2 * GDN_HEADS:],
                                  w_in_gdn[..., g4:g4 + 2 * GDN_HEADS]], axis=-1), GDN_C).astype(BF16)
    w_out_bf = w_out.astype(BF16)
    w_up_bf = w_up.astype(BF16)
    w_down_bf = w_down.astype(BF16)

    kv = layer_matmul(mem_prompt.reshape(n * mem_len, d), w_mem_kv.astype(BF16))
    mk_gain = jnp.tile(mem_k_norm.astype(F32), (1, LANES // HEAD_DIM)).reshape(depth, 1, LANES)
    mem_k_p = mem_key_norm(kv, mk_gain).reshape(depth, n, mem_len, MEM_W)
    mem_v_p = kv[..., MEM_W:].reshape(depth, n, mem_len, MEM_W)
    mem_k_s = cache_mem_k.reshape(depth, ns, mem_len, MEM_W)
    mem_v_s = cache_mem_v.reshape(depth, ns, mem_len, MEM_W)

    tm_ffn = _pick_tile(r, (640, 512, 256))
    tq_mem = _pick_tile(t, (512, 256))
    causal_new = jnp.where((jnp.arange(LANES)[None, :] <= jnp.arange(SAMPLE_ROWS)[:, None])
                           & (jnp.arange(LANES)[None, :] < ts), 0.0, NEG).astype(F32)
    causal_new = jnp.broadcast_to(causal_new, (ns, SAMPLE_ROWS, LANES))
    zero_key_bias = jnp.zeros((ns, SAMPLE_ROWS, past), F32)
    kt_all = jnp.transpose(cache_k, (0, 1, 3, 4, 2)).reshape(cache_k.shape[0], -1, ATT_W, PAGE_SIZE)
    vt_all = jnp.transpose(cache_v, (0, 1, 3, 4, 2)).reshape(cache_v.shape[0], -1, ATT_W, PAGE_SIZE)

    new_k, new_v, new_idx, new_delta_p, new_delta_s, new_conv_p, new_conv_s = [], [], [], [], [], [], []
    kind_count = [0, 0, 0]
    kv_slot = 0
    for l in range(depth):
        kind = l % N_MIXERS
        j = kind_count[kind]
        kind_count[kind] += 1
        proj = norm_matmul(x_all, norm_mix[l], (w_moba, w_dsa, w_gdn)[kind], j)
        if kind == 2:
            memq_col = 4 * GDN_W // MEM_W
            hp = jnp.zeros((2, LANES), F32)
            hp = hp.at[0, :GDN_HEADS].set(gdn_a_log[j].astype(F32)).at[1, :GDN_HEADS].set(
                gdn_dt_bias[j].astype(F32))
            on = gdn_out_norm[j].astype(F32).reshape(1, LANES)
            kwid = CONV_W - 1
            o_p, sd_p = gdn_mix(proj, gdn_conv[j], jnp.zeros((n, kwid, CONV_CH), F32), hp, on,
                                jnp.zeros((n, GDN_HEADS, GDN_DK, LANES), F32), 0, n, t,
                                MOBA_BLOCK, GDN_CHUNK, 3, None)
            o_s, sd_s = gdn_mix(proj, gdn_conv[j], state_conv[j], hp, on, state_delta[j].astype(F32),
                                rp, ns, SAMPLE_ROWS, SAMPLE_ROWS, SAMPLE_ROWS, GDN_HEADS, ts)
            new_delta_p.append(sd_p)
            new_delta_s.append(sd_s)
            new_conv_p.append(jnp.stack([proj[b * t + t - kwid:(b + 1) * t, :CONV_CH] for b in range(n)]))
            new_conv_s.append(proj[rp:rp + rs].reshape(ns, SAMPLE_ROWS, -1)[:, ts - kwid:ts, :CONV_CH])
        else:
            is_dsa = kind == 1
            memq_col = (3 * ATT_W + (MEM_W if is_dsa else 0)) // MEM_W
            outs = qk_post(proj, is_dsa, tabs, itabs, _lane_gain(attn_q_norm[kv_slot]),
                           _lane_gain(attn_k_norm[kv_slot]), rp)
            q_all, k_pr, k_sm, kmean, v_pr, v_sm = outs[:6]
            kmean = kmean.reshape(r // ROW_TILE, ATT_W)
            if is_dsa:
                qi_all, kit_all = outs[6], outs[7]
                mask = dsa_select_prompt(qi_all, kit_all, proj, n, t)
                o_p = prompt_attention(q_all, k_pr, proj, mask, True, n, t)
                key_bias, new_bias = dsa_select_sample(cache_idx[j], page_table, qi_all, kit_all,
                                                       proj, rp, ns, ts)
                new_idx.append((kit_all[:rp, :IDX_DIM], kit_all[rp:, :IDX_DIM]))
            else:
                o_p = prompt_attention(q_all, k_pr, proj, kmean, False, n, t)
                key_bias, new_bias = zero_key_bias, causal_new
            o_s = sample_attention(kt_all, vt_all, kv_slot, page_table, q_all, k_sm, v_sm,
                                   key_bias, new_bias, not is_dsa, rp, ns)
            new_k.append((k_pr, k_sm))
            new_v.append((v_pr, v_sm))
            kv_slot += 1
        mq_gain = _lane_gain(mem_q_norm[l])
        om_p = mem_attention(proj, memq_col, mq_gain, mem_k_p[l], mem_v_p[l], 0, n, t, tq_mem)
        om_s = mem_attention(proj, memq_col, mq_gain, mem_k_s[l], mem_v_s[l], rp, ns, SAMPLE_ROWS,
                             SAMPLE_ROWS)
        tail = r - rp - rs
        o_all = jnp.concatenate([o_p, o_s, jnp.zeros((tail, ATT_W), F32)], axis=0)
        om_all = jnp.concatenate([om_p, om_s, jnp.zeros((tail, MEM_W), F32)], axis=0)
        x_all = post_block(x_all, o_all, om_all, w_out_bf, norm_ffn[l], w_up_bf, w_down_bf, l,
                           tm_ffn, 2048)

    def split(pairs, width_shape):
        ap = jnp.stack([a for a, _ in pairs]).reshape(len(pairs), n, t, *width_shape)
        asmp = jnp.stack([a[:rs] for _, a in pairs]).reshape(
            len(pairs), ns, SAMPLE_ROWS, *width_shape)[:, :, :ts]
        return ap, asmp

    y_prompt = x_all[:rp].reshape(n, t, d)
    y_sample = x_all[rp:rp + rs].reshape(ns, SAMPLE_ROWS, d)[:, :ts]
    k_p, k_s = split(new_k, (N_HEADS, HEAD_DIM))
    v_p, v_s = split(new_v, (N_HEADS, HEAD_DIM))
    idx_p, idx_s = split(new_idx, (IDX_DIM,))
    mem_k_out = mem_k_p.reshape(depth, n, mem_len, MEM_HEADS, HEAD_DIM)
    mem_v_out = mem_v_p.reshape(depth, n, mem_len, MEM_HEADS, HEAD_DIM)
    return (y_prompt, y_sample, k_p, v_p, idx_p, mem_k_out, mem_v_out,
            jnp.stack(new_delta_p), jnp.stack(new_conv_p), k_s, v_s, idx_s,
            jnp.stack(new_delta_s), jnp.stack(new_conv_s))
```

```python
import functools
import math

import jax
import jax.numpy as jnp
from jax import lax
from jax.experimental import pallas as pl
from jax.experimental.pallas import tpu as pltpu

F32 = jnp.float32
BF16 = jnp.bfloat16
I32 = jnp.int32
HI = lax.Precision.HIGHEST

D_MODEL = 1024
HEAD_DIM = 64
N_HEADS = 12
ATT_W = N_HEADS * HEAD_DIM
MEM_HEADS = 4
MEM_W = MEM_HEADS * HEAD_DIM
ROT_DIM = HEAD_DIM // 4
ROPE_THETA = 500000.0
ATT_SCALE = HEAD_DIM ** -0.5
LOG2E = math.log2(math.e)
MOBA_BLOCK = 256
MOBA_TOPK = 3
IDX_HEADS = 8
IDX_DIM = 32
IDX_ROT = IDX_DIM // 4
DSA_TOPK = 256
GDN_HEADS = 6
GDN_DK = 128
GDN_W = GDN_HEADS * GDN_DK
CONV_W = 4
CONV_CH = 3 * GDN_W
GDN_CHUNK = 64
D_FF = 4 * D_MODEL
EPS = 1e-6
NEG = -1e30
PAGE_SIZE = 128
N_MIXERS = 3

LANES = 128
SUBLANES = 8
BF16_ROWS = 16
ROW_TILE = 256
SAMPLE_ROWS = 8
VMEM_LIMIT = 56 * 1024 * 1024
INT_MIN = -2 ** 31
KEYS_PER_ROW = LANES // IDX_DIM
SUB_SPACING_STEPS = (0.5, 0.25, 0.125, 0.0625)
PAGES_PER_STEP = 8

MOBA_C = 3 * ATT_W + MEM_W
DSA_C = 3 * ATT_W + 2 * MEM_W + LANES
GDN_C = 4 * GDN_W + MEM_W + LANES


def _cparams(sem):
    return pltpu.CompilerParams(dimension_semantics=sem, vmem_limit_bytes=VMEM_LIMIT)


def _dot_nt(a, b, precision=None):
    return lax.dot_general(a, b, (((1,), (1,)), ((), ())), precision=precision,
                           preferred_element_type=F32)


def _dot3(a, b, dims):
    a_hi = a.astype(BF16)
    b_hi = b.astype(BF16)
    a_lo = (a - a_hi.astype(F32)).astype(BF16)
    b_lo = (b - b_hi.astype(F32)).astype(BF16)
    dg = lambda x, y: lax.dot_general(x, y, dims, preferred_element_type=F32)
    return dg(a_hi, b_hi) + dg(a_hi, b_lo) + dg(a_lo, b_hi)


def _head_block_ones(width):
    r = lax.broadcasted_iota(I32, (LANES, LANES), 0) // width
    c = lax.broadcasted_iota(I32, (LANES, LANES), 1) // width
    return jnp.where(r == c, 1.0, 0.0).astype(BF16)


def _group_sum(x2, gmat):
    hi = x2.astype(BF16)
    lo = (x2 - hi.astype(F32)).astype(BF16)
    return (jnp.dot(hi, gmat, preferred_element_type=F32)
            + jnp.dot(lo, gmat, preferred_element_type=F32))


def _head_rms(x, gain, gmat, width):
    ms = _group_sum(x * x, gmat) * (1.0 / width)
    return x * lax.rsqrt(ms + EPS) * gain


def _rope(y, c, sa, sb, half):
    return y * c + pltpu.roll(y, LANES - half, 1) * sa + pltpu.roll(y, half, 1) * sb


def _norm_matmul_kernel(x_ref, g_ref, w_ref, o_ref):
    x = x_ref[...]
    ms = jnp.mean(x * x, axis=-1, keepdims=True)
    h = (x * lax.rsqrt(ms + EPS) * g_ref[...]).astype(BF16)
    o_ref[...] = jnp.dot(h, w_ref[...], preferred_element_type=F32)


def norm_matmul(x, g, w_all, j):
    r, d = x.shape
    c = w_all.shape[2]
    tm = ROW_TILE
    return pl.pallas_call(
        _norm_matmul_kernel,
        grid=(r // tm,),
        in_specs=[pl.BlockSpec((tm, d), lambda i: (i, 0)),
                  pl.BlockSpec((1, d), lambda i: (0, 0)),
                  pl.BlockSpec((None, d, c), lambda i: (j, 0, 0))],
        out_specs=pl.BlockSpec((tm, c), lambda i: (i, 0)),
        out_shape=jax.ShapeDtypeStruct((r, c), F32),
        compiler_params=_cparams(("parallel",)),
        name="norm_matmul",
    )(x, g.reshape(1, d), w_all)


def _matmul_kernel(x_ref, w_ref, o_ref):
    o_ref[...] = jnp.dot(x_ref[...].astype(BF16), w_ref[...], preferred_element_type=F32)


def layer_matmul(x, w):
    r, d = x.shape
    nl, _, c = w.shape
    return pl.pallas_call(
        _matmul_kernel,
        grid=(nl,),
        in_specs=[pl.BlockSpec((r, d), lambda l: (0, 0)),
                  pl.BlockSpec((None, d, c), lambda l: (l, 0, 0))],
        out_specs=pl.BlockSpec((None, r, c), lambda l: (l, 0, 0)),
        out_shape=jax.ShapeDtypeStruct((nl, r, c), F32),
        compiler_params=_cparams(("parallel",)),
        name="mem_kv_matmul",
    )(x, w)


def _mem_k_norm_kernel(kv_ref, g_ref, o_ref):
    gmat = _head_block_ones(HEAD_DIM)
    g = g_ref[...]
    for p in range(MEM_W // LANES):
        x = kv_ref[:, p * LANES:(p + 1) * LANES]
        o_ref[:, p * LANES:(p + 1) * LANES] = _head_rms(x, g, gmat, HEAD_DIM)


def mem_key_norm(kv, gains):
    nl, r, _ = kv.shape
    return pl.pallas_call(
        _mem_k_norm_kernel,
        grid=(nl,),
        in_specs=[pl.BlockSpec((None, r, MEM_W), lambda l: (l, 0, 0)),
                  pl.BlockSpec((None, 1, LANES), lambda l: (l, 0, 0))],
        out_specs=pl.BlockSpec((None, r, MEM_W), lambda l: (l, 0, 0)),
        out_shape=jax.ShapeDtypeStruct((nl, r, MEM_W), F32),
        compiler_params=_cparams(("parallel",)),
        name="mem_k_norm",
    )(kv, gains)


def _qk_post_kernel(is_dsa, np_tiles, *refs):
    if is_dsa:
        (q_ref, k_ref, v_ref, qi_ref, kw_ref, c_ref, sa_ref, sb_ref, ci_ref, sai_ref, sbi_ref,
         qg_ref, kg_ref, qo_ref, kop_ref, kos_ref, km_ref, vop_ref, vos_ref, qio_ref, kit_ref) = refs
    else:
        (q_ref, k_ref, v_ref, c_ref, sa_ref, sb_ref, qg_ref, kg_ref,
         qo_ref, kop_ref, kos_ref, km_ref, vop_ref, vos_ref) = refs
    is_prompt = pl.program_id(0) < np_tiles

    def put(p_ref, s_ref, sl, val):
        @pl.when(is_prompt)
        def _():
            p_ref[:, sl] = val

        @pl.when(jnp.logical_not(is_prompt))
        def _():
            s_ref[:, sl] = val

    put(vop_ref, vos_ref, slice(None), v_ref[...])
    gmat = _head_block_ones(HEAD_DIM)
    c, sa, sb = c_ref[...], sa_ref[...], sb_ref[...]
    qg, kg = qg_ref[...], kg_ref[...]
    half = ROT_DIM // 2
    for p in range(ATT_W // LANES):
        sl = slice(p * LANES, (p + 1) * LANES)
        qn = _rope(_head_rms(q_ref[:, sl], qg, gmat, HEAD_DIM), c, sa, sb, half)
        qo_ref[:, sl] = qn * (ATT_SCALE * LOG2E)
        kn = _rope(_head_rms(k_ref[:, sl], kg, gmat, HEAD_DIM), c, sa, sb, half)
        put(kop_ref, kos_ref, sl, kn)
        km_ref[:, sl] = jnp.sum(kn, axis=0, keepdims=True) * (1.0 / ROW_TILE)
    if is_dsa:
        ci, sai, sbi = ci_ref[...], sai_ref[...], sbi_ref[...]
        ihalf = IDX_ROT // 2
        for p in range(IDX_HEADS * IDX_DIM // LANES):
            sl = slice(p * LANES, (p + 1) * LANES)
            qio_ref[:, sl] = _rope(qi_ref[:, sl], ci, sai, sbi, ihalf)
        kw = kw_ref[...]
        lane = lax.broadcasted_iota(I32, kw.shape, 1)
        kz = jnp.where(lane < IDX_DIM, kw, 0.0)
        kt = kz
        for s in range(1, LANES // IDX_DIM):
            kt = kt + pltpu.roll(kz, s * IDX_DIM, 1)
        kit_ref[...] = _rope(kt, ci, sai, sbi, ihalf)


def qk_post(proj, is_dsa, tabs, itabs, q_gain, k_gain, rp):
    r = proj.shape[0]
    tm = ROW_TILE
    np_tiles = rp // tm
    assert rp % tm == 0 and r > rp
    row = lambda cb: (lambda i: (i, cb))
    prow = lambda i: (jnp.minimum(i, np_tiles - 1), 0)
    srow = lambda i: (jnp.maximum(i - np_tiles, 0), 0)
    group_specs = [pl.BlockSpec((tm, ATT_W), prow), pl.BlockSpec((tm, ATT_W), srow)]
    group_shapes = [jax.ShapeDtypeStruct((rp, ATT_W), F32), jax.ShapeDtypeStruct((r - rp, ATT_W), F32)]
    tab_spec = pl.BlockSpec((tm, LANES), lambda i: (i, 0))
    gain_spec = pl.BlockSpec((1, LANES), lambda i: (0, 0))
    in_specs = [pl.BlockSpec((tm, ATT_W), row(0)), pl.BlockSpec((tm, ATT_W), row(1)),
                pl.BlockSpec((tm, ATT_W), row(2))]
    args = [proj, proj, proj]
    if is_dsa:
        in_specs += [pl.BlockSpec((tm, MEM_W), row(3 * ATT_W // MEM_W)),
                     pl.BlockSpec((tm, LANES), row((3 * ATT_W + 2 * MEM_W) // LANES))]
        args += [proj, proj]
    in_specs += [tab_spec] * 3
    args += list(tabs)
    if is_dsa:
        in_specs += [tab_spec] * 3
        args += list(itabs)
    in_specs += [gain_spec, gain_spec]
    args += [q_gain, k_gain]
    out_specs = ([pl.BlockSpec((tm, ATT_W), row(0))] + group_specs
                 + [pl.BlockSpec((None, 1, ATT_W), lambda i: (i, 0, 0))] + group_specs)
    out_shape = ([jax.ShapeDtypeStruct((r, ATT_W), F32)] + group_shapes
                 + [jax.ShapeDtypeStruct((r // tm, 1, ATT_W), F32)] + group_shapes)
    if is_dsa:
        out_specs += [pl.BlockSpec((tm, MEM_W), row(0)), pl.BlockSpec((tm, LANES), row(0))]
        out_shape += [jax.ShapeDtypeStruct((r, MEM_W), F32), jax.ShapeDtypeStruct((r, LANES), F32)]
    return pl.pallas_call(
        functools.partial(_qk_post_kernel, is_dsa, np_tiles),
        grid=(r // tm,),
        in_specs=in_specs, out_specs=out_specs, out_shape=out_shape,
        compiler_params=_cparams(("arbitrary",)),
        name="qk_post_dsa" if is_dsa else "qk_post",
    )(*args)


def _post_kernel(x_ref, o_ref, om_ref, wa_ref, wb_ref, g_ref, wu_ref, wd_ref, y_ref, h_sc):
    f = pl.program_id(1)

    @pl.when(f == 0)
    def _():
        x1 = (x_ref[...]
              + jnp.dot(o_ref[...].astype(BF16), wa_ref[...], preferred_element_type=F32)
              + jnp.dot(om_ref[...].astype(BF16), wb_ref[...], preferred_element_type=F32))
        ms = jnp.mean(x1 * x1, axis=-1, keepdims=True)
        h_sc[...] = (x1 * lax.rsqrt(ms + EPS) * g_ref[...]).astype(BF16)
        y_ref[...] = x1

    up = jnp.dot(h_sc[...], wu_ref[...], preferred_element_type=F32)
    act = jnp.square(jnp.maximum(up, 0.0)).astype(BF16)
    y_ref[...] += jnp.dot(act, wd_ref[...], preferred_element_type=F32)


def post_block(x, o, om, w_out, g, w_up, w_down, l, tm, tf):
    r, d = x.shape
    return pl.pallas_call(
        _post_kernel,
        grid=(r // tm, D_FF // tf),
        in_specs=[pl.BlockSpec((tm, d), lambda i, f: (i, 0)),
                  pl.BlockSpec((tm, ATT_W), lambda i, f: (i, 0)),
                  pl.BlockSpec((tm, MEM_W), lambda i, f: (i, 0)),
                  pl.BlockSpec((None, ATT_W, d), lambda i, f: (l, 0, 0)),
                  pl.BlockSpec((None, MEM_W, d), lambda i, f: (l, ATT_W // MEM_W, 0)),
                  pl.BlockSpec((1, d), lambda i, f: (0, 0)),
                  pl.BlockSpec((None, d, tf), lambda i, f: (l, 0, f)),
                  pl.BlockSpec((None, tf, d), lambda i, f: (l, f, 0))],
        out_specs=pl.BlockSpec((tm, d), lambda i, f: (i, 0)),
        out_shape=jax.ShapeDtypeStruct((r, d), F32),
        scratch_shapes=[pltpu.VMEM((tm, d), BF16)],
        compiler_params=_cparams(("parallel", "arbitrary")),
        name="out_proj_ffn",
    )(x, o, om, w_out, w_out, g.reshape(1, d), w_up, w_down)


def _topk_block_bias(gate, n_past, n_top):
    nbk = gate.shape[1]
    col = lax.broadcasted_iota(I32, gate.shape, 1)
    past = col < n_past
    g = jnp.where(past, gate, NEG)
    sel = jnp.zeros(gate.shape, F32)
    for _ in range(n_top):
        mx = jnp.max(g, axis=1, keepdims=True)
        idx = jnp.min(jnp.where(g == mx, col, nbk), axis=1, keepdims=True)
        pick = col == idx
        sel = jnp.where(pick, 1.0, sel)
        g = jnp.where(pick, -jnp.inf, g)
    return jnp.where(past, jnp.where(sel > 0.5, 0.0, NEG), NEG)


def _topk_block_bias_t(gate, n_past, n_top):
    nbk = gate.shape[0]
    row = lax.broadcasted_iota(I32, gate.shape, 0)
    past = row < n_past
    g = jnp.where(past, gate, NEG)
    sel = jnp.zeros(gate.shape, F32)
    for _ in range(n_top):
        mx = jnp.max(g, axis=0, keepdims=True)
        idx = jnp.min(jnp.where(g == mx, row, nbk), axis=0, keepdims=True)
        pick = row == idx
        sel = jnp.where(pick, 1.0, sel)
        g = jnp.where(pick, -jnp.inf, g)
    return jnp.where(past, jnp.where(sel > 0.5, 0.0, NEG), NEG)


def _prompt_attn_kernel(is_dsa, nb, n_top, q_ref, k_ref, v_ref, aux_ref, o_ref,
                        kbf_sc, vt_sc, m_sc, acc_sc, bias_sc):
    blk = MOBA_BLOCK
    i = pl.program_id(2)

    @pl.when(i == 0)
    def _():
        kbf_sc[...] = k_ref[...].astype(BF16)
        ones_row = jnp.where(lax.broadcasted_iota(I32, (BF16_ROWS, blk), 0) == 0, 1.0, 0.0)
        for c in range(nb):
            vt_sc[c, 0:LANES, :] = jnp.transpose(v_ref[c * blk:(c + 1) * blk, :]).astype(BF16)
            vt_sc[c, LANES:LANES + BF16_ROWS, :] = ones_row.astype(BF16)

    q = q_ref[...]
    lane = lax.broadcasted_iota(I32, q.shape, 1)
    qcat = jnp.concatenate([jnp.where(lane // HEAD_DIM == hh, q, 0.0) for hh in range(2)], axis=0)
    qcat_bf = qcat.astype(BF16)
    m_sc[...] = jnp.full(m_sc.shape, NEG, F32)
    acc_sc[...] = jnp.zeros(acc_sc.shape, F32)

    def scores(j):
        st = _dot_nt(kbf_sc[pl.ds(pl.multiple_of(j * blk, blk), blk), :], qcat_bf)
        if is_dsa:
            mb = aux_ref[j].astype(F32)
            st = st + jnp.concatenate([mb, mb], axis=1)
        return st

    def update(js, sts, col_bias):
        m_old = m_sc[...]
        m_new = m_old
        for st, cb in zip(sts, col_bias):
            mx = jnp.max(st, axis=0, keepdims=True)
            m_new = jnp.maximum(m_new, mx if cb is None else mx + cb)
        acc = jnp.exp2(m_old - m_new) * acc_sc[...]
        for j, st, cb in zip(js, sts, col_bias):
            p = jnp.exp2((st - (m_new if cb is None else m_new - cb)).astype(BF16))
            acc = acc + jnp.dot(vt_sc[j], p, preferred_element_type=F32)
        acc_sc[...] = acc
        m_sc[...] = m_new

    if is_dsa:
        col_bias_of = lambda j: None
        count = i + 1
    else:
        gate = _dot_nt(aux_ref[...], qcat, HI)
        bias_sc[...] = _topk_block_bias_t(gate, i, n_top)
        kr = lax.broadcasted_iota(I32, (blk, 2 * blk), 0)
        qc = lax.broadcasted_iota(I32, (blk, 2 * blk), 1) % blk
        update([i], [scores(i) + jnp.where(kr <= qc, 0.0, NEG)], [None])
        col_bias_of = lambda j: bias_sc[pl.ds(j, 1), :]
        count = i

    npairs = count // 2

    def pair(jj, sts):
        nxt = jnp.minimum(jj + 1, npairs - 1)
        nxt_sts = (scores(2 * nxt), scores(2 * nxt + 1))
        js = [2 * jj, 2 * jj + 1]
        update(js, sts, [col_bias_of(j) for j in js])
        return nxt_sts

    lax.fori_loop(0, npairs, pair, (scores(0), scores(1)))

    @pl.when(count % 2 == 1)
    def _():
        update([count - 1], [scores(count - 1)], [col_bias_of(count - 1)])

    acc = acc_sc[...]
    res = acc[0:LANES, :] / acc[LANES:LANES + 1, :]
    drow = lax.broadcasted_iota(I32, (LANES, blk), 0)
    o_t = jnp.where(drow // HEAD_DIM == 0, res[:, :blk], res[:, blk:])
    o_ref[...] = jnp.transpose(o_t)


def prompt_attention(q_all, k_all, proj, aux, is_dsa, n, t):
    blk = MOBA_BLOCK
    nb = t // blk
    npair = ATT_W // LANES
    n_top = min(MOBA_TOPK, nb)
    vcol = 2 * ATT_W // LANES
    if is_dsa:
        aux_spec = pl.BlockSpec((None, None, nb, blk, blk), lambda b, p, i: (b, i, 0, 0, 0))
    else:
        aux_spec = pl.BlockSpec((nb, LANES), lambda b, p, i: (b, p))
    return pl.pallas_call(
        functools.partial(_prompt_attn_kernel, is_dsa, nb, n_top),
        grid=(n, npair, nb),
        in_specs=[pl.BlockSpec((blk, LANES), lambda b, p, i: (b * nb + i, p)),
                  pl.BlockSpec((t, LANES), lambda b, p, i: (b, p)),
                  pl.BlockSpec((t, LANES), lambda b, p, i: (b, vcol + p)),
                  aux_spec],
        out_specs=pl.BlockSpec((blk, LANES), lambda b, p, i: (b * nb + i, p)),
        out_shape=jax.ShapeDtypeStruct((n * t, ATT_W), F32),
        scratch_shapes=[pltpu.VMEM((t, LANES), BF16), pltpu.VMEM((nb, LANES + BF16_ROWS, blk), BF16),
                        pltpu.VMEM((1, 2 * blk), F32),
                        pltpu.VMEM((LANES + BF16_ROWS, 2 * blk), F32), pltpu.VMEM((nb, 2 * blk), F32)],
        compiler_params=_cparams(("parallel", "parallel", "arbitrary")),
        name="prompt_attn_dsa" if is_dsa else "prompt_attn_moba",
    )(q_all, k_all, proj, aux)


def _key_to_float(k):
    b = jnp.where(k < 0, k ^ jnp.int32(0x7FFFFFFF), k)
    return lax.bitcast_convert_type(b, F32)


def _select_topk(count_ge, count_gt, count_tie_lt, k_sel, shape, idx_bits):
    def vbit(b, t):
        cand = t + lax.shift_left(jnp.int32(1), 31 - b)
        return jnp.where(count_ge(_key_to_float(cand)) >= k_sel, cand, t)

    vkey = lax.fori_loop(0, 32, vbit, jnp.full(shape, INT_MIN, I32))
    v = _key_to_float(vkey)
    ulp = _key_to_float(vkey + 1) - v
    for frac in SUB_SPACING_STEPS:
        cand = v + ulp * frac
        v = jnp.where(count_ge(cand) >= k_sel, cand, v)
    need = k_sel - count_gt(v)

    def cbit(b, c):
        cand = c + lax.shift_left(jnp.int32(1), idx_bits - 1 - b)
        return jnp.where(count_tie_lt(v, cand) <= need - 1, cand, c)

    all_ties_taken = jnp.all(count_ge(v) == k_sel)
    c = lax.cond(all_ties_taken,
                 lambda: jnp.full(shape, (1 << idx_bits) - 1, I32),
                 lambda: lax.fori_loop(0, idx_bits, cbit, jnp.zeros(shape, I32)))
    return v, c


def _dsa_select_kernel(nb, k_sel, idx_bits, qi_ref, kit_ref, w_ref, o_ref, sc_sc):
    blk = MOBA_BLOCK
    i = pl.program_id(1)
    w_t = jnp.transpose(w_ref[...])
    lane = lax.broadcasted_iota(I32, (blk, LANES), 1)
    qm, wrow = [], []
    for h in range(IDX_HEADS):
        seg = qi_ref[:, (h // 4) * LANES:(h // 4 + 1) * LANES]
        qm.append(jnp.where(lane // IDX_DIM == h % 4, seg, 0.0).astype(BF16))
        wrow.append(w_t[IDX_DIM + h:IDX_DIM + h + 1, :] * (1.0 / 16.0))
    kr = lax.broadcasted_iota(I32, (blk, blk), 0)
    qpos = i * blk + lax.broadcasted_iota(I32, (blk, blk), 1)

    def score_block(c, carry):
        off = pl.multiple_of(c * blk, blk)
        kt = kit_ref[pl.ds(off, blk), :].astype(BF16)
        acc = jnp.zeros((blk, blk), F32)
        for h in range(IDX_HEADS):
            acc = acc + jnp.maximum(_dot_nt(kt, qm[h]), 0.0) * wrow[h]
        sc_sc[c] = jnp.where(c * blk + kr <= qpos, acc + 0.0, -jnp.inf)
        return carry

    lax.fori_loop(0, i + 1, score_block, 0)

    def fold(x):
        return jnp.sum(x.reshape(blk // SUBLANES, SUBLANES, blk), axis=0)

    def count_ge(cand):
        def body(c, acc):
            return acc + fold(jnp.where(sc_sc[c] >= cand, 1, 0))

        acc = lax.fori_loop(0, i + 1, body, jnp.zeros((SUBLANES, blk), I32))
        return jnp.sum(acc, axis=0, keepdims=True)

    def count_gt(cand):
        def body(c, acc):
            return acc + fold(jnp.where(sc_sc[c] > cand, 1, 0))

        acc = lax.fori_loop(0, i + 1, body, jnp.zeros((SUBLANES, blk), I32))
        return jnp.sum(acc, axis=0, keepdims=True)

    def count_tie_lt(v, cut):
        def body(c, acc):
            hit = (sc_sc[c] == v) & (c * blk + kr < cut)
            return acc + fold(jnp.where(hit, 1, 0))

        acc = lax.fori_loop(0, i + 1, body, jnp.zeros((SUBLANES, blk), I32))
        return jnp.sum(acc, axis=0, keepdims=True)

    v, cut = _select_topk(count_ge, count_gt, count_tie_lt, k_sel, (1, blk), idx_bits)
    take_all = i * blk + lax.broadcasted_iota(I32, (1, blk), 1) + 1 <= k_sel

    def emit(c, carry):
        s = sc_sc[c]
        kpos = c * blk + kr
        bias = jnp.where(s > v, 0.0, jnp.where((s == v) & (kpos <= cut), 0.0, NEG))
        bias = jnp.where(take_all, 0.0, bias)
        o_ref[c] = jnp.where(kpos <= qpos, bias, NEG).astype(BF16)
        return carry

    lax.fori_loop(0, i + 1, emit, 0)

    def fill(c, carry):
        o_ref[c] = jnp.full((blk, blk), NEG, BF16)
        return carry

    lax.fori_loop(i + 1, nb, fill, 0)


def dsa_select_prompt(qi, kit, proj, n, t):
    blk = MOBA_BLOCK
    nb = t // blk
    k_sel = min(DSA_TOPK, t // 4)
    idx_bits = max(1, (t - 1).bit_length())
    wcol = (3 * ATT_W + 2 * MEM_W) // LANES
    return pl.pallas_call(
        functools.partial(_dsa_select_kernel, nb, k_sel, idx_bits),
        grid=(n, nb),
        in_specs=[pl.BlockSpec((blk, MEM_W), lambda b, i: (b * nb + i, 0)),
                  pl.BlockSpec((t, LANES), lambda b, i: (b, 0)),
                  pl.BlockSpec((blk, LANES), lambda b, i: (b * nb + i, wcol))],
        out_specs=pl.BlockSpec((None, None, nb, blk, blk), lambda b, i: (b, i, 0, 0, 0)),
        out_shape=jax.ShapeDtypeStruct((n, nb, nb, blk, blk), BF16),
        scratch_shapes=[pltpu.VMEM((nb, blk, blk), F32)],
        compiler_params=_cparams(("parallel", "arbitrary")),
        name="dsa_select_prompt",
    )(qi, kit, proj)


def _mem_attn_kernel(q_ref, g_ref, mk_ref, mv_ref, o_ref):
    gmat = _head_block_ones(HEAD_DIM)
    g = g_ref[...]
    for p in range(MEM_W // LANES):
        sl = slice(p * LANES, (p + 1) * LANES)
        q = _head_rms(q_ref[:, sl], g, gmat, HEAD_DIM) * ATT_SCALE
        lane = lax.broadcasted_iota(I32, q.shape, 1)
        kb = mk_ref[:, sl].astype(BF16)
        vb = mv_ref[:, sl].astype(BF16)
        outs = []
        for hh in range(2):
            qh = jnp.where(lane // HEAD_DIM == hh, q, 0.0).astype(BF16)
            s = _dot_nt(qh, kb)
            pexp = jnp.exp(s - jnp.max(s, axis=1, keepdims=True))
            den = jnp.sum(pexp, axis=1, keepdims=True)
            outs.append(jnp.dot(pexp.astype(BF16), vb, preferred_element_type=F32) / den)
        o_ref[:, sl] = jnp.where(lane // HEAD_DIM == 0, outs[0], outs[1])


def mem_attention(proj, qcol, gain, mk, mv, row0, n, t, tq):
    nq = t // tq
    rb0 = row0 // tq
    ml = mk.shape[1]
    return pl.pallas_call(
        _mem_attn_kernel,
        grid=(n, nq),
        in_specs=[pl.BlockSpec((tq, MEM_W), lambda b, i: (rb0 + b * nq + i, qcol)),
                  pl.BlockSpec((1, LANES), lambda b, i: (0, 0)),
                  pl.BlockSpec((None, ml, MEM_W), lambda b, i: (b, 0, 0)),
                  pl.BlockSpec((None, ml, MEM_W), lambda b, i: (b, 0, 0))],
        out_specs=pl.BlockSpec((tq, MEM_W), lambda b, i: (b * nq + i, 0)),
        out_shape=jax.ShapeDtypeStruct((n * t, MEM_W), F32),
        compiler_params=_cparams(("parallel", "parallel")),
        name="mem_attn",
    )(proj, gain, mk, mv)


def _tile_rows(x, reps):
    return jnp.concatenate([x] * reps, axis=0)


def _block_diag_queries(q8):
    qt = _tile_rows(q8, N_HEADS)
    row = lax.broadcasted_iota(I32, qt.shape, 0)
    lane = lax.broadcasted_iota(I32, qt.shape, 1)
    return jnp.where(lane // HEAD_DIM == row // SAMPLE_ROWS, qt, 0.0)


def _sample_attn_kernel(nsteps, nbp, n_top, use_gate, *refs):
    npp = PAGES_PER_STEP
    pt_ref = refs[0]
    k_refs, v_refs = refs[1:1 + npp], refs[1 + npp:1 + 2 * npp]
    (q_ref, kn_ref, vn_ref, km_ref, nbias_ref, o_ref,
     qbd_sc, gate_sc, m_sc, l_sc, acc_sc) = refs[1 + 2 * npp:]
    j = pl.program_id(1)
    ppb = MOBA_BLOCK // PAGE_SIZE

    @pl.when(j == 0)
    def _():
        qbd_sc[...] = _block_diag_queries(q_ref[...])
        gate_sc[...] = jnp.zeros(gate_sc.shape, F32)

    qbd_bf = qbd_sc[...].astype(BF16)
    kmask = km_ref[...]
    lane = lax.broadcasted_iota(I32, (N_HEADS * SAMPLE_ROWS, LANES), 1)
    for blk in range(npp // ppb):
        bi = (npp // ppb) * j + blk
        scores = [jnp.dot(qbd_bf, k_refs[blk * ppb + pg][...].astype(BF16),
                          preferred_element_type=F32) for pg in range(ppb)]
        s = jnp.concatenate(scores, axis=1)
        if use_gate:
            gate_b = jnp.sum(s, axis=1, keepdims=True) * (1.0 / MOBA_BLOCK)
            gate_sc[...] += jnp.where(lane == bi, gate_b, 0.0)
        s = s + _tile_rows(kmask[:, blk * MOBA_BLOCK:(blk + 1) * MOBA_BLOCK], N_HEADS)
        m = jnp.max(s, axis=1, keepdims=True)
        p = jnp.exp2(s - m)
        pb = p.astype(BF16)
        acc = jnp.zeros((N_HEADS * SAMPLE_ROWS, ATT_W), F32)
        for pg in range(ppb):
            vb = v_refs[blk * ppb + pg][...].astype(BF16)
            acc = acc + _dot_nt(pb[:, pg * PAGE_SIZE:(pg + 1) * PAGE_SIZE], vb)
        m_sc[bi] = m
        l_sc[bi] = jnp.sum(p, axis=1, keepdims=True)
        acc_sc[bi] = acc

    @pl.when(j == nsteps - 1)
    def _():
        pad = jnp.zeros((PAGE_SIZE - SAMPLE_ROWS, ATT_W), F32)
        kn = jnp.concatenate([kn_ref[...], pad], axis=0).astype(BF16)
        vn = jnp.concatenate([vn_ref[...], pad], axis=0).astype(BF16)
        s = _dot_nt(qbd_bf, kn) + _tile_rows(nbias_ref[...], N_HEADS)
        m_n = jnp.max(s, axis=1, keepdims=True)
        p = jnp.exp2(s - m_n)
        l_n = jnp.sum(p, axis=1, keepdims=True)
        acc_n = jnp.dot(p.astype(BF16), vn, preferred_element_type=F32)
        if use_gate:
            bias = _topk_block_bias(gate_sc[...], nbp, n_top)
            mb = [m_sc[b] + bias[:, b:b + 1] for b in range(nbp)]
        else:
            mb = [m_sc[b] for b in range(nbp)]
        m_all = m_n
        for b in range(nbp):
            m_all = jnp.maximum(m_all, mb[b])
        w_n = jnp.exp2(m_n - m_all)
        den = w_n * l_n
        num = w_n * acc_n
        for b in range(nbp):
            w_b = jnp.exp2(mb[b] - m_all)
            den = den + w_b * l_sc[b]
            num = num + w_b * acc_sc[b]
        res = num / den
        olane = lax.broadcasted_iota(I32, (SAMPLE_ROWS, ATT_W), 1)
        out = jnp.zeros((SAMPLE_ROWS, ATT_W), F32)
        for h in range(N_HEADS):
            out = out + jnp.where(olane // HEAD_DIM == h,
                                  res[h * SAMPLE_ROWS:(h + 1) * SAMPLE_ROWS, :], 0.0)
        o_ref[...] = out


def sample_attention(kt_all, vt_all, slot, page_table, q_all, k_new, v_new, key_bias, new_bias,
                     use_gate, row0, ns):
    npg = page_table.shape[1]
    npp = PAGES_PER_STEP
    assert npg % npp == 0
    nsteps = npg // npp
    nbp = npg // (MOBA_BLOCK // PAGE_SIZE)
    assert nbp <= LANES
    n_top = min(MOBA_TOPK, nbp + 1)
    rb0 = row0 // SAMPLE_ROWS
    rows = N_HEADS * SAMPLE_ROWS
    page = lambda off: pl.BlockSpec((None, None, ATT_W, PAGE_SIZE),
                                    lambda b, j, pt: (slot, pt[b, npp * j + off], 0, 0))
    rowblk = lambda cb: pl.BlockSpec((SAMPLE_ROWS, ATT_W), lambda b, j, pt: (rb0 + b, cb))
    newblk = pl.BlockSpec((SAMPLE_ROWS, ATT_W), lambda b, j, pt: (b, 0))
    grid_spec = pltpu.PrefetchScalarGridSpec(
        num_scalar_prefetch=1,
        grid=(ns, nsteps),
        in_specs=[page(off) for off in range(npp)] * 2 + [
            rowblk(0), newblk, newblk,
            pl.BlockSpec((None, SAMPLE_ROWS, npp * PAGE_SIZE), lambda b, j, pt: (b, 0, j)),
            pl.BlockSpec((None, SAMPLE_ROWS, LANES), lambda b, j, pt: (b, 0, 0))],
        out_specs=pl.BlockSpec((SAMPLE_ROWS, ATT_W), lambda b, j, pt: (b, 0)),
        scratch_shapes=[pltpu.VMEM((rows, ATT_W), F32), pltpu.VMEM((rows, LANES), F32),
                        pltpu.VMEM((nbp, rows, 1), F32), pltpu.VMEM((nbp, rows, 1), F32),
                        pltpu.VMEM((nbp, rows, ATT_W), F32)],
    )
    return pl.pallas_call(
        functools.partial(_sample_attn_kernel, nsteps, nbp, n_top, use_gate),
        grid_spec=grid_spec,
        out_shape=jax.ShapeDtypeStruct((ns * SAMPLE_ROWS, ATT_W), F32),
        compiler_params=_cparams(("parallel", "arbitrary")),
        name="sample_attn_moba" if use_gate else "sample_attn_dsa",
    )(page_table, *([kt_all] * npp), *([vt_all] * npp), q_all, k_new, v_new, key_bias, new_bias)


def _dsa_select_sample_kernel(npg, ts, k_sel, idx_bits, pt_ref, cidx_ref, qi_ref, kit_ref, w_ref,
                              km_ref, nb_ref, kbuf, sem):
    b = pl.program_id(0)
    past = npg * PAGE_SIZE
    kpr = PAGE_SIZE // KEYS_PER_ROW
    nrow = past // KEYS_PER_ROW

    def page_copy(p):
        return pltpu.make_async_copy(cidx_ref.at[pt_ref[b, p]], kbuf.at[pl.ds(p * kpr, kpr)], sem)

    for p in range(npg):
        page_copy(p).start()
    for p in range(npg):
        page_copy(p).wait()

    lane = lax.broadcasted_iota(I32, (SAMPLE_ROWS, LANES), 1)
    wblk = w_ref[...]
    kpast = kbuf[...].astype(BF16)
    knew = jnp.where(lane < IDX_DIM, kit_ref[...], 0.0)
    knew = jnp.concatenate([knew, jnp.zeros((LANES - SAMPLE_ROWS, LANES), F32)], axis=0).astype(BF16)
    isc_p = [jnp.zeros((SAMPLE_ROWS, nrow), F32) for _ in range(KEYS_PER_ROW)]
    isc_n = jnp.zeros((SAMPLE_ROWS, LANES), F32)
    for h in range(IDX_HEADS):
        seg = qi_ref[:, (h // 4) * LANES:(h // 4 + 1) * LANES]
        w_h = jnp.sum(jnp.where(lane == IDX_DIM + h, wblk, 0.0), axis=1, keepdims=True) * (1.0 / 16.0)
        for c in range(KEYS_PER_ROW):
            shift = ((c - h % 4) * IDX_DIM) % LANES
            qs = pltpu.roll(seg, shift, 1) if shift else seg
            qmh = jnp.where(lane // IDX_DIM == c, qs, 0.0).astype(BF16)
            isc_p[c] = isc_p[c] + jnp.maximum(_dot_nt(qmh, kpast), 0.0) * w_h
            if c == 0:
                isc_n = isc_n + jnp.maximum(_dot_nt(qmh, knew), 0.0) * w_h
    key_p = [x + 0.0 for x in isc_p]
    trow = lax.broadcasted_iota(I32, (SAMPLE_ROWS, LANES), 0)
    valid_n = (lane <= trow) & (lane < ts)
    key_n = jnp.where(valid_n, isc_n + 0.0, -jnp.inf)
    rid = lax.broadcasted_iota(I32, (SAMPLE_ROWS, nrow), 1)
    idx_p = [KEYS_PER_ROW * rid + c for c in range(KEYS_PER_ROW)]
    idx_n = past + lane
    cnt = lambda m: jnp.sum(jnp.where(m, 1, 0), axis=1, keepdims=True)

    def count_ge(cand):
        return sum([cnt(k >= cand) for k in key_p], cnt(key_n >= cand))

    def count_gt(cand):
        return sum([cnt(k > cand) for k in key_p], cnt(key_n > cand))

    def count_tie_lt(v, cut):
        return sum([cnt((k == v) & (ix < cut)) for k, ix in zip(key_p, idx_p)],
                   cnt((key_n == v) & (idx_n < cut)))

    v, cut = _select_topk(count_ge, count_gt, count_tie_lt, k_sel, (SAMPLE_ROWS, 1), idx_bits)
    trow1 = lax.broadcasted_iota(I32, (SAMPLE_ROWS, 1), 0)
    take_all = past + jnp.minimum(trow1, ts - 1) + 1 <= k_sel
    for c in range(KEYS_PER_ROW):
        sel = (key_p[c] > v) | ((key_p[c] == v) & (idx_p[c] <= cut))
        km_ref[c] = jnp.where(take_all, 0.0, jnp.where(sel, 0.0, NEG))
    sel_n = (key_n > v) | ((key_n == v) & (idx_n <= cut))
    nb_ref[...] = jnp.where(valid_n, jnp.where(take_all, 0.0, jnp.where(sel_n, 0.0, NEG)), NEG)


def dsa_select_sample(cache_idx_slot, page_table, qi, kit, proj, row0, ns, ts):
    npg = page_table.shape[1]
    past = npg * PAGE_SIZE
    k_sel = min(DSA_TOPK, (past + ts) // 4)
    idx_bits = max(1, (past + LANES - 1).bit_length())
    rb0 = row0 // SAMPLE_ROWS
    wcol = (3 * ATT_W + 2 * MEM_W) // LANES
    nrow = past // KEYS_PER_ROW
    pages = cache_idx_slot.reshape(-1, PAGE_SIZE // KEYS_PER_ROW, LANES)
    grid_spec = pltpu.PrefetchScalarGridSpec(
        num_scalar_prefetch=1,
        grid=(ns,),
        in_specs=[pl.BlockSpec(memory_space=pl.ANY),
                  pl.BlockSpec((SAMPLE_ROWS, MEM_W), lambda b, pt: (rb0 + b, 0)),
                  pl.BlockSpec((SAMPLE_ROWS, LANES), lambda b, pt: (rb0 + b, 0)),
                  pl.BlockSpec((SAMPLE_ROWS, LANES), lambda b, pt: (rb0 + b, wcol))],
        out_specs=[pl.BlockSpec((None, KEYS_PER_ROW, SAMPLE_ROWS, nrow), lambda b, pt: (b, 0, 0, 0)),
                   pl.BlockSpec((None, SAMPLE_ROWS, LANES), lambda b, pt: (b, 0, 0))],
        scratch_shapes=[pltpu.VMEM((nrow, LANES), F32), pltpu.SemaphoreType.DMA(())],
    )
    km, nbias = pl.pallas_call(
        functools.partial(_dsa_select_sample_kernel, npg, ts, k_sel, idx_bits),
        grid_spec=grid_spec,
        out_shape=[jax.ShapeDtypeStruct((ns, KEYS_PER_ROW, SAMPLE_ROWS, nrow), F32),
                   jax.ShapeDtypeStruct((ns, SAMPLE_ROWS, LANES), F32)],
        compiler_params=_cparams(("arbitrary",)),
        name="dsa_select_sample",
    )(page_table, pages, qi, kit, proj)
    return jnp.transpose(km, (0, 2, 3, 1)).reshape(ns, SAMPLE_ROWS, past), nbias


def _gdn_local_kernel(tc, cs, hps, t_valid, q_ref, k_ref, v_ref, ba_ref, cwq_ref, cwk_ref, cwv_ref,
                      cbq_ref, cbk_ref, cbv_ref, hp_ref, u_ref, w_ref, qd_ref, kd_ref, at_ref, xp_sc):
    h0 = pl.program_id(1) * hps
    c = pl.program_id(2)
    kw = CONV_W - 1

    @pl.when(c == 0)
    def _():
        for seg, cb in enumerate((cbq_ref, cbk_ref, cbv_ref)):
            xp_sc[seg, SUBLANES - kw:SUBLANES, :] = cb[...]

    acts = []
    for seg, (xr, cw) in enumerate(((q_ref, cwq_ref), (k_ref, cwk_ref), (v_ref, cwv_ref))):
        xp_sc[seg, SUBLANES:SUBLANES + tc, :] = xr[...]
        y = xp_sc[seg, SUBLANES - kw:SUBLANES - kw + tc, :] * cw[0:1, :]
        for jj in range(1, CONV_W):
            y = y + xp_sc[seg, SUBLANES - kw + jj:SUBLANES - kw + jj + tc, :] * cw[jj:jj + 1, :]
        acts.append(jax.nn.silu(y))
        xp_sc[seg, SUBLANES - kw:SUBLANES, :] = xp_sc[seg, SUBLANES + tc - kw:SUBLANES + tc, :]
    lane = lax.broadcasted_iota(I32, (tc, LANES), 1)
    lane1 = lax.broadcasted_iota(I32, (1, LANES), 1)
    ba = ba_ref[...]
    hp = hp_ref[...]
    qn, kn, va, beta, g = [], [], [], [], []
    for hh in range(hps):
        cl = slice(hh * LANES, (hh + 1) * LANES)
        h = h0 + hh
        qa, ka = acts[0][:, cl], acts[1][:, cl]
        qn.append(qa * lax.rsqrt(jnp.sum(qa * qa, axis=-1, keepdims=True) + EPS) * (GDN_DK ** -0.5))
        kn.append(ka * lax.rsqrt(jnp.sum(ka * ka, axis=-1, keepdims=True) + EPS))
        va.append(acts[2][:, cl])
        bcol = jnp.sum(jnp.where(lane == h, ba, 0.0), axis=1, keepdims=True)
        acol = jnp.sum(jnp.where(lane == GDN_HEADS + h, ba, 0.0), axis=1, keepdims=True)
        alog = jnp.sum(jnp.where(lane1 == h, hp[0:1, :], 0.0), axis=1, keepdims=True)
        dtb = jnp.sum(jnp.where(lane1 == h, hp[1:2, :], 0.0), axis=1, keepdims=True)
        beta_h = jax.nn.sigmoid(bcol)
        g_h = -jnp.exp(alog) * jax.nn.softplus(acol + dtb)
        if t_valid is not None:
            rowid = c * tc + lax.broadcasted_iota(I32, (tc, 1), 0)
            beta_h = jnp.where(rowid < t_valid, beta_h, 0.0)
            g_h = jnp.where(rowid < t_valid, g_h, 0.0)
        beta.append(beta_h)
        g.append(g_h)

    ri = lax.broadcasted_iota(I32, (cs, cs), 0)
    ci = lax.broadcasted_iota(I32, (cs, cs), 1)
    tri = ri >= ci
    stri = ri > ci
    tril = jnp.where(tri, 1.0, 0.0)
    eye = jnp.where(ri == ci, 1.0, 0.0)
    e0 = jnp.where(lax.broadcasted_iota(I32, (cs, LANES), 1) == 0, 1.0, 0.0)
    mm = lambda a, b: _dot3(a, b, (((1,), (0,)), ((), ())))
    mm_nt = lambda a, b: _dot3(a, b, (((1,), (1,)), ((), ())))
    n_sq = max(0, (cs - 1).bit_length() - 1)

    units = [(hh, slice(ch * cs, (ch + 1) * cs)) for hh in range(hps) for ch in range(tc // cs)]
    gcum = [jnp.dot(tril, jnp.broadcast_to(g[hh][rs], (cs, LANES)), precision=HI,
                    preferred_element_type=F32) for hh, rs in units]
    grow = [_dot_nt(e0, x, HI) for x in gcum]
    decay = [jnp.where(tri, jnp.exp(jnp.where(tri, gc[:, 0:cs] - gr, 0.0)), 0.0)
             for gc, gr in zip(gcum, grow)]
    kb = [kn[hh][rs] * beta[hh][rs] for hh, rs in units]
    pw = [-jnp.where(stri, mm_nt(kbc, kn[hh][rs]) * dc, 0.0)
          for kbc, (hh, rs), dc in zip(kb, units, decay)]
    tinv = [eye + p for p in pw]
    for _ in range(n_sq):
        pw = [mm(p, p) for p in pw]
        tinv = [t + mm(t, p) for t, p in zip(tinv, pw)]
    egc = [jnp.exp(x) for x in gcum]
    for i, (hh, rs) in enumerate(units):
        cl = slice(hh * LANES, (hh + 1) * LANES)
        glast = gcum[i][cs - 1:cs, :]
        u_ref[rs, cl] = mm(tinv[i], va[hh][rs] * beta[hh][rs])
        w_ref[rs, cl] = mm(tinv[i], kb[i] * egc[i])
        qd_ref[rs, cl] = qn[hh][rs] * egc[i]
        kd_ref[rs, cl] = kn[hh][rs] * jnp.exp(glast - gcum[i])
        at = jnp.where(tri, mm_nt(qn[hh][rs], kn[hh][rs]) * decay[i], 0.0)
        at_ref[rs, cl] = jnp.zeros((cs, LANES), F32)
        at_ref[rs, hh * LANES:hh * LANES + cs] = at
        at_ref[rs, hh * LANES + LANES // 2:(hh + 1) * LANES] = jnp.broadcast_to(
            jnp.exp(glast), (cs, LANES))[:, LANES // 2:]


def _gdn_scan_kernel(tc, cs, u_ref, w_ref, qd_ref, kd_ref, at_ref, z_ref, on_ref, s0_ref,
                     o_ref, so_ref, s_sc):
    c = pl.program_id(1)
    nc = pl.num_programs(1)

    @pl.when(c == 0)
    def _():
        s_sc[...] = s0_ref[...]

    mm = lambda a, b: _dot3(a, b, (((1,), (0,)), ((), ())))
    lane = lax.broadcasted_iota(I32, (cs, LANES), 1)
    on = on_ref[...]
    heads = [slice(h * LANES, (h + 1) * LANES) for h in range(GDN_HEADS)]
    state = [s_sc[h] for h in range(GDN_HEADS)]
    for ch in range(tc // cs):
        rs = slice(ch * cs, (ch + 1) * cs)
        at = [at_ref[rs, cl] for cl in heads]
        v_new = [u_ref[rs, cl] - mm(w_ref[rs, cl], s) for cl, s in zip(heads, state)]
        o = [mm(qd_ref[rs, cl], s) + mm(a[:, 0:cs], vn)
             for cl, s, a, vn in zip(heads, state, at, v_new)]
        decay_tot = [jnp.where(lane < LANES // 2, pltpu.roll(a, LANES // 2, 1), a)[0:1, :] for a in at]
        state = [s * dt + _dot3(kd_ref[rs, cl], vn, (((0,), (0,)), ((), ())))
                 for cl, s, dt, vn in zip(heads, state, decay_tot, v_new)]
        for cl, oh in zip(heads, o):
            ms = jnp.mean(oh * oh, axis=-1, keepdims=True)
            o_ref[rs, cl] = oh * lax.rsqrt(ms + EPS) * on * jax.nn.silu(z_ref[rs, cl])
    for h in range(GDN_HEADS):
        s_sc[h] = state[h]

    @pl.when(c == nc - 1)
    def _():
        so_ref[...] = s_sc[...]


def gdn_mix(proj, conv_w, conv_buf, head_params, out_norm, s0, row0, n, t, tc, cs, hps, t_valid):
    nc = t // tc
    rb0 = row0 // tc
    bacol = (4 * GDN_W + MEM_W) // LANES
    kw = CONV_W - 1
    assert cs <= LANES // 2 and GDN_HEADS % hps == 0
    hw = hps * LANES
    hb = GDN_W // hw
    seg_spec = lambda s: pl.BlockSpec((tc, hw), lambda b, h, c: (rb0 + b * nc + c, s * hb + h))
    cw_spec = lambda s: pl.BlockSpec((CONV_W, hw), lambda b, h, c: (0, s * hb + h))
    cb_spec = lambda s: pl.BlockSpec((None, kw, hw), lambda b, h, c: (b, 0, s * hb + h))
    loc_spec = pl.BlockSpec((tc, hw), lambda b, h, c: (b * nc + c, h))
    loc_shape = jax.ShapeDtypeStruct((n * t, GDN_W), F32)
    local = pl.pallas_call(
        functools.partial(_gdn_local_kernel, tc, cs, hps, t_valid),
        grid=(n, hb, nc),
        in_specs=[seg_spec(0), seg_spec(1), seg_spec(2),
                  pl.BlockSpec((tc, LANES), lambda b, h, c: (rb0 + b * nc + c, bacol)),
                  cw_spec(0), cw_spec(1), cw_spec(2), cb_spec(0), cb_spec(1), cb_spec(2),
                  pl.BlockSpec((2, LANES), lambda b, h, c: (0, 0))],
        out_specs=[loc_spec] * 5,
        out_shape=[loc_shape] * 5,
        scratch_shapes=[pltpu.VMEM((3, SUBLANES + tc, hw), F32)],
        compiler_params=_cparams(("parallel", "parallel", "arbitrary")),
        name="gdn_local",
    )(proj, proj, proj, proj, conv_w, conv_w, conv_w, conv_buf, conv_buf, conv_buf, head_params)
    wide = pl.BlockSpec((tc, GDN_W), lambda b, c: (b * nc + c, 0))
    state_spec = pl.BlockSpec((None, GDN_HEADS, GDN_DK, LANES), lambda b, c: (b, 0, 0, 0))
    return pl.pallas_call(
        functools.partial(_gdn_scan_kernel, tc, cs),
        grid=(n, nc),
        in_specs=[wide] * 5 + [
            pl.BlockSpec((tc, GDN_W), lambda b, c: (rb0 + b * nc + c, 3)),
            pl.BlockSpec((1, LANES), lambda b, c: (0, 0)),
            state_spec],
        out_specs=[wide, state_spec],
        out_shape=[loc_shape, jax.ShapeDtypeStruct((n, GDN_HEADS, GDN_DK, LANES), F32)],
        scratch_shapes=[pltpu.VMEM((GDN_HEADS, GDN_DK, LANES), F32)],
        compiler_params=_cparams(("parallel", "arbitrary")),
        name="gdn_scan",
    )(*local, proj, out_norm, s0)


def _rope_tables(pos, rot_dim, head_dim):
    half = rot_dim // 2
    inv_freq = ROPE_THETA ** (-jnp.arange(0, rot_dim, 2, dtype=F32) / rot_dim)
    ang = pos.astype(F32)[:, None] * inv_freq[None, :]
    cos, sin = jnp.cos(ang), jnp.sin(ang)
    r = pos.shape[0]
    rest = head_dim - 2 * half
    zh = jnp.zeros((r, half), F32)
    c = jnp.concatenate([cos, cos, jnp.ones((r, rest), F32)], axis=1)
    sa = jnp.concatenate([-sin, zh, jnp.zeros((r, rest), F32)], axis=1)
    sb = jnp.concatenate([zh, sin, jnp.zeros((r, rest), F32)], axis=1)
    reps = LANES // head_dim
    return tuple(jnp.tile(a, (1, reps)) for a in (c, sa, sb))


def _lane_gain(g):
    return jnp.tile(g.astype(F32), LANES // g.shape[0]).reshape(1, LANES)


def _pick_tile(r, options):
    for tm in options:
        if r % tm == 0:
            return tm
    raise ValueError(f"no row tile for {r}")


def kernel(x_prompt, x_sample, mem_prompt, cache_k, cache_v, cache_idx, cache_mem_k, cache_mem_v,
           state_delta, state_conv, page_table, norm_mix, norm_ffn, attn_q_norm, attn_k_norm,
           mem_q_norm, mem_k_norm, w_in_moba, w_in_dsa, w_in_gdn, gdn_conv, gdn_a_log, gdn_dt_bias,
           gdn_out_norm, w_mem_kv, w_out, w_up, w_down):
    n, t, d = x_prompt.shape
    ns, ts, _ = x_sample.shape
    depth = norm_mix.shape[0]
    npg = page_table.shape[1]
    past = npg * PAGE_SIZE
    mem_len = mem_prompt.shape[1]
    assert d == D_MODEL and t % MOBA_BLOCK == 0 and t >= CONV_W - 1
    assert CONV_W - 1 <= ts <= SAMPLE_ROWS and past % MOBA_BLOCK == 0
    rp = n * t
    rs = ns * SAMPLE_ROWS
    r = -(-(rp + rs) // ROW_TILE) * ROW_TILE
    page_table = page_table.astype(I32)

    xs = jnp.pad(x_sample, ((0, 0), (0, SAMPLE_ROWS - ts), (0, 0))).reshape(rs, d)
    x_all = jnp.concatenate([x_prompt.reshape(rp, d), xs, jnp.zeros((r - rp - rs, d), F32)], axis=0)
    pos_s = past + jnp.minimum(jnp.arange(SAMPLE_ROWS, dtype=I32), ts - 1)
    pos_all = jnp.concatenate([jnp.tile(jnp.arange(t, dtype=I32), n), jnp.tile(pos_s, ns),
                               jnp.zeros((r - rp - rs,), I32)])
    tabs = _rope_tables(pos_all, ROT_DIM, HEAD_DIM)
    itabs = _rope_tables(pos_all, IDX_ROT, IDX_DIM)

    zpad = lambda w, c: jnp.pad(w, ((0, 0), (0, 0), (0, c - w.shape[-1])))
    a3 = 3 * ATT_W
    qi_w = IDX_HEADS * IDX_DIM
    kw_w = IDX_DIM + IDX_HEADS
    w_moba = w_in_moba.astype(BF16)
    w_dsa = zpad(jnp.concatenate([w_in_dsa[..., :a3 + qi_w], w_in_dsa[..., a3 + qi_w + kw_w:],
                                  w_in_dsa[..., a3 + qi_w:a3 + qi_w + kw_w]], axis=-1), DSA_C).astype(BF16)
    g4 = 4 * GDN_W
    w_gdn = zpad(jnp.concatenate([w_in_gdn[..., :g4], w_in_gdn[..., g4 + 2 * GDN_HEADS:],
                                  w_in_gdn[..., g4:g4 + 2 * GDN_HEADS]], axis=-1), GDN_C).astype(BF16)
    w_out_bf = w_out.astype(BF16)
    w_up_bf = w_up.astype(BF16)
    w_down_bf = w_down.astype(BF16)

    kv = layer_matmul(mem_prompt.reshape(n * mem_len, d), w_mem_kv.astype(BF16))
    mk_gain = jnp.tile(mem_k_norm.astype(F32), (1, LANES // HEAD_DIM)).reshape(depth, 1, LANES)
    mem_k_p = mem_key_norm(kv, mk_gain).reshape(depth, n, mem_len, MEM_W)
    mem_v_p = kv[..., MEM_W:].reshape(depth, n, mem_len, MEM_W)
    mem_k_s = cache_mem_k.reshape(depth, ns, mem_len, MEM_W)
    mem_v_s = cache_mem_v.reshape(depth, ns, mem_len, MEM_W)

    tm_ffn = _pick_tile(r, (640, 512, 256))
    tq_mem = _pick_tile(t, (512, 256))
    causal_new = jnp.where((jnp.arange(LANES)[None, :] <= jnp.arange(SAMPLE_ROWS)[:, None])
                           & (jnp.arange(LANES)[None, :] < ts), 0.0, NEG).astype(F32)
    causal_new = jnp.broadcast_to(causal_new, (ns, SAMPLE_ROWS, LANES))
    zero_key_bias = jnp.zeros((ns, SAMPLE_ROWS, past), F32)
    kt_all = jnp.transpose(cache_k, (0, 1, 3, 4, 2)).reshape(cache_k.shape[0], -1, ATT_W, PAGE_SIZE)
    vt_all = jnp.transpose(cache_v, (0, 1, 3, 4, 2)).reshape(cache_v.shape[0], -1, ATT_W, PAGE_SIZE)

    new_k, new_v, new_idx, new_delta_p, new_delta_s, new_conv_p, new_conv_s = [], [], [], [], [], [], []
    kind_count = [0, 0, 0]
    kv_slot = 0
    for l in range(depth):
        kind = l % N_MIXERS
        j = kind_count[kind]
        kind_count[kind] += 1
        proj = norm_matmul(x_all, norm_mix[l], (w_moba, w_dsa, w_gdn)[kind], j)
        if kind == 2:
            memq_col = 4 * GDN_W // MEM_W
            hp = jnp.zeros((2, LANES), F32)
            hp = hp.at[0, :GDN_HEADS].set(gdn_a_log[j].astype(F32)).at[1, :GDN_HEADS].set(
                gdn_dt_bias[j].astype(F32))
            on = gdn_out_norm[j].astype(F32).reshape(1, LANES)
            kwid = CONV_W - 1
            o_p, sd_p = gdn_mix(proj, gdn_conv[j], jnp.zeros((n, kwid, CONV_CH), F32), hp, on,
                                jnp.zeros((n, GDN_HEADS, GDN_DK, LANES), F32), 0, n, t,
                                MOBA_BLOCK, GDN_CHUNK, 3, None)
            o_s, sd_s = gdn_mix(proj, gdn_conv[j], state_conv[j], hp, on, state_delta[j].astype(F32),
                                rp, ns, SAMPLE_ROWS, SAMPLE_ROWS, SAMPLE_ROWS, GDN_HEADS, ts)
            new_delta_p.append(sd_p)
            new_delta_s.append(sd_s)
            new_conv_p.append(jnp.stack([proj[b * t + t - kwid:(b + 1) * t, :CONV_CH] for b in range(n)]))
            new_conv_s.append(proj[rp:rp + rs].reshape(ns, SAMPLE_ROWS, -1)[:, ts - kwid:ts, :CONV_CH])
        else:
            is_dsa = kind == 1
            memq_col = (3 * ATT_W + (MEM_W if is_dsa else 0)) // MEM_W
            outs = qk_post(proj, is_dsa, tabs, itabs, _lane_gain(attn_q_norm[kv_slot]),
                           _lane_gain(attn_k_norm[kv_slot]), rp)
            q_all, k_pr, k_sm, kmean, v_pr, v_sm = outs[:6]
            kmean = kmean.reshape(r // ROW_TILE, ATT_W)
            if is_dsa:
                qi_all, kit_all = outs[6], outs[7]
                mask = dsa_select_prompt(qi_all, kit_all, proj, n, t)
                o_p = prompt_attention(q_all, k_pr, proj, mask, True, n, t)
                key_bias, new_bias = dsa_select_sample(cache_idx[j], page_table, qi_all, kit_all,
                                                       proj, rp, ns, ts)
                new_idx.append((kit_all[:rp, :IDX_DIM], kit_all[rp:, :IDX_DIM]))
            else:
                o_p = prompt_attention(q_all, k_pr, proj, kmean, False, n, t)
                key_bias, new_bias = zero_key_bias, causal_new
            o_s = sample_attention(kt_all, vt_all, kv_slot, page_table, q_all, k_sm, v_sm,
                                   key_bias, new_bias, not is_dsa, rp, ns)
            new_k.append((k_pr, k_sm))
            new_v.append((v_pr, v_sm))
            kv_slot += 1
        mq_gain = _lane_gain(mem_q_norm[l])
        om_p = mem_attention(proj, memq_col, mq_gain, mem_k_p[l], mem_v_p[l], 0, n, t, tq_mem)
        om_s = mem_attention(proj, memq_col, mq_gain, mem_k_s[l], mem_v_s[l], rp, ns, SAMPLE_ROWS,
                             SAMPLE_ROWS)
        tail = r - rp - rs
        o_all = jnp.concatenate([o_p, o_s, jnp.zeros((tail, ATT_W), F32)], axis=0)
        om_all = jnp.concatenate([om_p, om_s, jnp.zeros((tail, MEM_W), F32)], axis=0)
        x_all = post_block(x_all, o_all, om_all, w_out_bf, norm_ffn[l], w_up_bf, w_down_bf, l,
                           tm_ffn, 2048)

    def split(pairs, width_shape):
        ap = jnp.stack([a for a, _ in pairs]).reshape(len(pairs), n, t, *width_shape)
        asmp = jnp.stack([a[:rs] for _, a in pairs]).reshape(
            len(pairs), ns, SAMPLE_ROWS, *width_shape)[:, :, :ts]
        return ap, asmp

    y_prompt = x_all[:rp].reshape(n, t, d)
    y_sample = x_all[rp:rp + rs].reshape(ns, SAMPLE_ROWS, d)[:, :ts]
    k_p, k_s = split(new_k, (N_HEADS, HEAD_DIM))
    v_p, v_s = split(new_v, (N_HEADS, HEAD_DIM))
    idx_p, idx_s = split(new_idx, (IDX_DIM,))
    mem_k_out = mem_k_p.reshape(depth, n, mem_len, MEM_HEADS, HEAD_DIM)
    mem_v_out = mem_v_p.reshape(depth, n, mem_len, MEM_HEADS, HEAD_DIM)
    return (y_prompt, y_sample, k_p, v_p, idx_p, mem_k_out, mem_v_out,
            jnp.stack(new_delta_p), jnp.stack(new_conv_p), k_s, v_s, idx_s,
            jnp.stack(new_delta_s), jnp.stack(new_conv_s))
```
